```python
import math
import jax, jax.numpy as jnp
from jax import lax
import numpy as np

D_MODEL = 1024
BATCH = 16
SEQ = 2048
DEPTH = 4
DEC_BATCH = 128
DEC_SEQ = 8
PAST_LEN = 8192
PAGE_SIZE = 128

N_MEM = 256
MLA_HEADS = 8
QK_NOPE = 128
QK_ROPE = 64
V_HEAD = 128
Q_RANK = 384
KV_RANK = 256
MLA_WIDTH = MLA_HEADS * V_HEAD
RG_WIDTH = D_MODEL // 2
RG_BLOCKS = 8
RG_BLOCK = RG_WIDTH // RG_BLOCKS
CONV_W = 4
RG_C = 8.0
X_HEADS = 4
X_HEAD_DIM = 128
X_WIDTH = X_HEADS * X_HEAD_DIM
MIX_WIDTH = MLA_WIDTH + RG_WIDTH + X_WIDTH
IN_SPLITS = (Q_RANK, KV_RANK, QK_ROPE, MLA_WIDTH, RG_WIDTH, RG_WIDTH, X_WIDTH, X_WIDTH)
IN_WIDTH = Q_RANK + KV_RANK + QK_ROPE + MLA_WIDTH + 2 * RG_WIDTH + 2 * X_WIDTH
ROPE_BASE = 10000.0
EPS = 1e-6
Q_BLOCK = 128

kernel_name = 'hymba_mla_rglru_memx_step'


def rms_norm(x, g):
    xf = x.astype(jnp.float32)
    y = xf * lax.rsqrt(jnp.mean(xf * xf, axis=-1, keepdims=True) + EPS)
    return (y * g.astype(jnp.float32)).astype(x.dtype)


def rope(x, pos):
    half = x.shape[-1] // 2
    inv = 1.0 / (ROPE_BASE ** (jnp.arange(half, dtype=jnp.float32) / half))
    ang = pos.astype(jnp.float32)[:, None, None] * inv[None, None, :]
    cos, sin = jnp.cos(ang), jnp.sin(ang)
    xf = x.astype(jnp.float32)
    x1, x2 = xf[..., :half], xf[..., half:]
    return jnp.concatenate([x1 * cos - x2 * sin, x2 * cos + x1 * sin], axis=-1).astype(x.dtype)


def mla_attend(q_c, q_pe, c_k, kpe_k, q_pos, k_pos):
    B, T, H, R = q_c.shape
    P = q_pe.shape[-1]
    scale = (QK_NOPE + QK_ROPE) ** -0.5
    qb = Q_BLOCK if T % Q_BLOCK == 0 else T
    nb = T // qb

    def block(args):
        qc, qp, pos = args
        s = (jnp.einsum('bthr,bsr->bhts', qc, c_k, preferred_element_type=jnp.float32)
             + jnp.einsum('bthp,bsp->bhts', qp, kpe_k, preferred_element_type=jnp.float32)) * scale
        mask = k_pos[None, :] <= pos[:, None]
        s = jnp.where(mask[None, None], s, -jnp.inf)
        p = jax.nn.softmax(s, axis=-1).astype(c_k.dtype)
        return jnp.einsum('bhts,bsr->bthr', p, c_k)

    qc_b = q_c.reshape(B, nb, qb, H, R).transpose(1, 0, 2, 3, 4)
    qp_b = q_pe.reshape(B, nb, qb, H, P).transpose(1, 0, 2, 3, 4)
    out = lax.map(block, (qc_b, qp_b, q_pos.reshape(nb, qb)))
    return out.transpose(1, 0, 2, 3, 4).reshape(B, T, H, R)


def rg_lru(x, h0, w_r, b_r, w_i, b_i, lam):
    B, T, W = x.shape
    xb = x.reshape(B, T, RG_BLOCKS, RG_BLOCK)
    r = jax.nn.sigmoid(jnp.einsum('btnj,njk->btnk', xb, w_r) + b_r).reshape(B, T, W)
    i = jax.nn.sigmoid(jnp.einsum('btnj,njk->btnk', xb, w_i) + b_i).reshape(B, T, W)
    log_a = -RG_C * r.astype(jnp.float32) * jax.nn.softplus(-lam.astype(jnp.float32))
    a = jnp.exp(log_a)
    bx = jnp.sqrt(-jnp.expm1(2.0 * log_a)) * (i * x).astype(jnp.float32)

    def step(h, ab):
        a_t, b_t = ab
        h = a_t * h + b_t
        return h, h

    h_last, hs = lax.scan(step, h0.astype(jnp.float32), (a.transpose(1, 0, 2), bx.transpose(1, 0, 2)))
    return hs.transpose(1, 0, 2).astype(x.dtype), h_last.astype(h0.dtype)


def mem_kv_proj(mem, g, w):
    B, M, _ = mem.shape
    kv = rms_norm(mem, g) @ w
    k = kv[..., :X_WIDTH].reshape(B, M, X_HEADS, X_HEAD_DIM)
    v = kv[..., X_WIDTH:].reshape(B, M, X_HEADS, X_HEAD_DIM)
    return k, v


def hybrid_layer(x, pos, c_past, kpe_past, k_pos_past, h0, conv0, mem_k, mem_v,
                 ln_g, w_in, q_norm_g, kv_norm_g, w_q_b, w_kv_b, conv_w, conv_b,
                 w_r, b_r, w_i, b_i, rg_lambda, w_out):
    B, T, _ = x.shape
    u = rms_norm(x, ln_g)
    hcat = u @ w_in
    pts = np.cumsum(IN_SPLITS)[:-1].tolist()
    q_lat, c_raw, kpe_raw, g_a, x_rg, g_rg, q_x, g_x = jnp.split(hcat, pts, axis=-1)

    q = jnp.einsum('btr,rhd->bthd', rms_norm(q_lat, q_norm_g), w_q_b)
    q_nope = q[..., :QK_NOPE]
    q_pe = rope(q[..., QK_NOPE:], pos)
    c_new = rms_norm(c_raw, kv_norm_g)
    kpe_new = rope(kpe_raw[:, :, None, :], pos)[:, :, 0]
    w_uk = w_kv_b[..., :QK_NOPE]
    w_uv = w_kv_b[..., QK_NOPE:]
    q_c = jnp.einsum('bthd,rhd->bthr', q_nope, w_uk)
    if c_past is None:
        c_all, kpe_all, k_pos = c_new, kpe_new, pos
    else:
        c_all = jnp.concatenate([c_past, c_new], axis=1)
        kpe_all = jnp.concatenate([kpe_past, kpe_new], axis=1)
        k_pos = jnp.concatenate([k_pos_past, pos], axis=0)
    o_lat = mla_attend(q_c, q_pe, c_all, kpe_all, pos, k_pos)
    o_a = jnp.einsum('bthr,rhd->bthd', o_lat, w_uv).reshape(B, T, MLA_WIDTH)
    y_a = o_a * jax.nn.silu(g_a)

    xp = jnp.concatenate([conv0, x_rg], axis=1)
    xc = conv_b + sum(xp[:, k:k + T] * conv_w[k] for k in range(CONV_W))
    conv_new = xp[:, T:]
    y_rg, h_last = rg_lru(xc, h0, w_r, b_r, w_i, b_i, rg_lambda)
    y_b = y_rg * jax.nn.silu(g_rg)

    qx = q_x.reshape(B, T, X_HEADS, X_HEAD_DIM)
    s = jnp.einsum('bthd,bmhd->bhtm', qx, mem_k, preferred_element_type=jnp.float32) * (X_HEAD_DIM ** -0.5)
    pr = jax.nn.softmax(s, axis=-1).astype(mem_v.dtype)
    o_x = jnp.einsum('bhtm,bmhd->bthd', pr, mem_v).reshape(B, T, X_WIDTH)
    y_c = o_x * jax.nn.silu(g_x)

    out = jnp.concatenate([y_a, y_b, y_c], axis=-1) @ w_out
    return x + out, c_new, kpe_new, h_last, conv_new


def setup_inputs(seed: int = 0) -> dict:
    key = jax.random.key(seed)
    ks = jax.random.split(key, 32)
    f32 = jnp.float32
    n_pages = PAST_LEN // PAGE_SIZE
    n_used = DEC_BATCH * n_pages
    n_phys = n_used + n_used // 4
    perm = jax.random.permutation(ks[0], n_phys)
    page_table = perm[:n_used].reshape(DEC_BATCH, n_pages).astype(jnp.int32)

    def nrm(k, shape, scale):
        return jax.random.normal(k, shape, f32) * scale

    a_eff = jax.random.uniform(ks[1], (DEPTH, RG_WIDTH), f32, minval=0.9, maxval=0.999)
    a_base = a_eff ** (1.0 / RG_C)
    rg_lambda = jnp.log(a_base) - jnp.log1p(-a_base)

    return {
        'x_prompt': nrm(ks[2], (BATCH, SEQ, D_MODEL), 1.0),
        'x_sample': nrm(ks[3], (DEC_BATCH, DEC_SEQ, D_MODEL), 1.0),
        'mem_prompt': nrm(ks[4], (BATCH, N_MEM, D_MODEL), 1.0),
        'cache_ckv': nrm(ks[5], (DEPTH, n_phys, PAGE_SIZE, KV_RANK), 1.0),
        'cache_kpe': nrm(ks[6], (DEPTH, n_phys, PAGE_SIZE, QK_ROPE), 1.0),
        'state_rg_h': nrm(ks[7], (DEPTH, DEC_BATCH, RG_WIDTH), 0.5),
        'state_rg_conv': nrm(ks[8], (DEPTH, DEC_BATCH, CONV_W - 1, RG_WIDTH), 1.0),
        'cache_mem_k': nrm(ks[9], (DEPTH, DEC_BATCH, N_MEM, X_HEADS, X_HEAD_DIM), 1.0),
        'cache_mem_v': nrm(ks[10], (DEPTH, DEC_BATCH, N_MEM, X_HEADS, X_HEAD_DIM), 1.0),
        'page_table': page_table,
        'ln_g': 1.0 + nrm(ks[11], (DEPTH, D_MODEL), 0.02),
        'w_in': nrm(ks[12], (DEPTH, D_MODEL, IN_WIDTH), D_MODEL ** -0.5),
        'q_norm_g': 1.0 + nrm(ks[13], (DEPTH, Q_RANK), 0.02),
        'kv_norm_g': 1.0 + nrm(ks[14], (DEPTH, KV_RANK), 0.02),
        'w_q_b': nrm(ks[15], (DEPTH, Q_RANK, MLA_HEADS, QK_NOPE + QK_ROPE), Q_RANK ** -0.5),
        'w_kv_b': nrm(ks[16], (DEPTH, KV_RANK, MLA_HEADS, QK_NOPE + V_HEAD), KV_RANK ** -0.5),
        'conv_w': nrm(ks[17], (DEPTH, CONV_W, RG_WIDTH), CONV_W ** -0.5),
        'conv_b': nrm(ks[18], (DEPTH, RG_WIDTH), 0.01),
        'w_r': nrm(ks[19], (DEPTH, RG_BLOCKS, RG_BLOCK, RG_BLOCK), RG_BLOCK ** -0.5),
        'b_r': nrm(ks[20], (DEPTH, RG_BLOCKS, RG_BLOCK), 0.01),
        'w_i': nrm(ks[21], (DEPTH, RG_BLOCKS, RG_BLOCK, RG_BLOCK), RG_BLOCK ** -0.5),
        'b_i': nrm(ks[22], (DEPTH, RG_BLOCKS, RG_BLOCK), 0.01),
        'rg_lambda': rg_lambda,
        'mem_norm_g': 1.0 + nrm(ks[23], (DEPTH, D_MODEL), 0.02),
        'w_mem_kv': nrm(ks[24], (DEPTH, D_MODEL, 2 * X_WIDTH), D_MODEL ** -0.5),
        'w_out': nrm(ks[25], (DEPTH, MIX_WIDTH, D_MODEL), MIX_WIDTH ** -0.5),
        'final_norm_g': 1.0 + nrm(ks[26], (D_MODEL,), 0.02),
    }


def reference(x_prompt, x_sample, mem_prompt, cache_ckv, cache_kpe, state_rg_h, state_rg_conv,
              cache_mem_k, cache_mem_v, page_table, ln_g, w_in, q_norm_g, kv_norm_g, w_q_b, w_kv_b,
              conv_w, conv_b, w_r, b_r, w_i, b_i, rg_lambda, mem_norm_g, w_mem_kv, w_out, final_norm_g):
    B, S, _ = x_prompt.shape
    Bd, T, _ = x_sample.shape
    n_pages = page_table.shape[1]
    past = n_pages * PAGE_SIZE
    pos_p = jnp.arange(S, dtype=jnp.int32)
    pos_s = past + jnp.arange(T, dtype=jnp.int32)
    past_pos = jnp.arange(past, dtype=jnp.int32)

    xp, xs = x_prompt, x_sample
    ckv_p, kpe_p, h_p, cv_p, mk_p, mv_p = [], [], [], [], [], []
    ckv_s, kpe_s, h_s, cv_s = [], [], [], []
    for l in range(DEPTH):
        lw = (ln_g[l], w_in[l], q_norm_g[l], kv_norm_g[l], w_q_b[l], w_kv_b[l], conv_w[l], conv_b[l],
              w_r[l], b_r[l], w_i[l], b_i[l], rg_lambda[l], w_out[l])
        mk, mv = mem_kv_proj(mem_prompt, mem_norm_g[l], w_mem_kv[l])
        h0 = jnp.zeros((B, RG_WIDTH), xp.dtype)
        conv0 = jnp.zeros((B, CONV_W - 1, RG_WIDTH), xp.dtype)
        xp, c, kpe, h, cv = hybrid_layer(xp, pos_p, None, None, None, h0, conv0, mk, mv, *lw)
        ckv_p.append(c); kpe_p.append(kpe); h_p.append(h); cv_p.append(cv); mk_p.append(mk); mv_p.append(mv)
        c_past = cache_ckv[l, page_table].reshape(Bd, past, KV_RANK)
        kpe_past = cache_kpe[l, page_table].reshape(Bd, past, QK_ROPE)
        xs, c, kpe, h, cv = hybrid_layer(xs, pos_s, c_past, kpe_past, past_pos, state_rg_h[l], state_rg_conv[l],
                                         cache_mem_k[l], cache_mem_v[l], *lw)
        ckv_s.append(c); kpe_s.append(kpe); h_s.append(h); cv_s.append(cv)

    y_prompt = rms_norm(xp, final_norm_g)
    y_sample = rms_norm(xs, final_norm_g)
    return (y_prompt, y_sample,
            jnp.stack(ckv_p), jnp.stack(kpe_p), jnp.stack(h_p), jnp.stack(cv_p), jnp.stack(mk_p), jnp.stack(mv_p),
            jnp.stack(ckv_s), jnp.stack(kpe_s), jnp.stack(h_s), jnp.stack(cv_s))
```

```python
import functools

import jax
import jax.numpy as jnp
from jax import lax
from jax.experimental import pallas as pl
from jax.experimental.pallas import tpu as pltpu

EPS = 1e-6
ROPE_BASE = 10000.0
RG_C = 8.0
LANES = 128
SUBLANES = 8
NEG_BIG = -1e30
VMEM_LIMIT = 56 * 1024 * 1024

F32 = jnp.float32
BF16 = jnp.bfloat16

_NT = (((1,), (1,)), ((), ()))


def _params(n_grid, vmem=VMEM_LIMIT):
    return pltpu.CompilerParams(dimension_semantics=("arbitrary",) * n_grid, vmem_limit_bytes=vmem)


def _full(shape):
    n = len(shape)
    return pl.BlockSpec(shape, lambda *_: (0,) * n)


def _rms(x, g):
    return x * lax.rsqrt(jnp.mean(x * x, axis=-1, keepdims=True) + EPS) * g


def _sigmoid(x):
    return 1.0 / (1.0 + jnp.exp(-x))


def _silu(x):
    return x * _sigmoid(x)


def _dot(a, b):
    return jnp.dot(a, b, preferred_element_type=F32)


def _dot_nt(a, b):
    return lax.dot_general(a, b, _NT, preferred_element_type=F32)


def _rope_padded(x, cos_t, sin_t, half):
    swapped = pltpu.roll(x, half, 1) + pltpu.roll(x, LANES - half, 1)
    return x * cos_t + swapped * sin_t


def _in_proj_kernel(*refs, decode, n_heads, nope, half, scale, x_scale):
    (x_ref, lng_ref, cos_ref, sin_ref, wq_ref, qg_ref, wqb_ref, wc_ref, cg_ref, wk_ref,
     wa_ref, wxr_ref, wgr_ref, wqx_ref, wgx_ref) = refs[:15]
    if decode:
        wukt_ref = refs[15]
        (qc_ref, qpe_ref, c_ref, kpe_ref, sga_ref, xrg_ref, sgr_ref, qx_ref, sgx_ref) = refs[16:]
    else:
        wuk_ref, wuv_ref = refs[15:17]
        (qf_ref, c_ref, kpe_ref, kn_ref, kpb_ref, v_ref, sga_ref, xrg_ref, sgr_ref, qx_ref,
         sgx_ref) = refs[17:]

    u = _rms(x_ref[...], lng_ref[...]).astype(BF16)
    cos_t = cos_ref[...]
    sin_t = sin_ref[...]

    qn = _rms(_dot(u, wq_ref[...]), qg_ref[...]).astype(BF16)
    hw = nope + LANES
    for h in range(n_heads):
        qh = _dot(qn, wqb_ref[:, h * hw:(h + 1) * hw])
        q_nope = qh[:, :nope] * scale
        q_pe = _rope_padded(qh[:, nope:], cos_t, sin_t, half) * scale
        if decode:
            qc_ref[:, h * wukt_ref.shape[2]:(h + 1) * wukt_ref.shape[2]] = _dot(q_nope.astype(BF16), wukt_ref[h])
            qpe_ref[:, h * LANES:(h + 1) * LANES] = q_pe
        else:
            qf_ref[:, h * hw:h * hw + nope] = q_nope.astype(BF16)
            qf_ref[:, h * hw + nope:(h + 1) * hw] = q_pe.astype(BF16)

    c_new = _rms(_dot(u, wc_ref[...]), cg_ref[...])
    c_ref[...] = c_new
    kpe = _rope_padded(_dot(u, wk_ref[...]), cos_t, sin_t, half)
    kpe_ref[...] = kpe[:, :kpe_ref.shape[1]]
    if not decode:
        cb = c_new.astype(BF16)
        kn_ref[...] = _dot(cb, wuk_ref[...]).astype(BF16)
        v_ref[...] = _dot(cb, wuv_ref[...]).astype(BF16)
        kpb_ref[...] = kpe.astype(BF16)

    sga_ref[...] = _silu(_dot(u, wa_ref[...])).astype(sga_ref.dtype)
    xrg_ref[...] = _dot(u, wxr_ref[...])
    sgr_ref[...] = _silu(_dot(u, wgr_ref[...])).astype(sgr_ref.dtype)
    qx_ref[...] = (_dot(u, wqx_ref[...]) * x_scale).astype(qx_ref.dtype)
    sgx_ref[...] = _silu(_dot(u, wgx_ref[...])).astype(sgx_ref.dtype)


def _in_proj(x, lw, cos_t, sin_t, *, decode, tm, seq):
    n, d = x.shape
    n_heads, nope, rank, rope = lw["n_heads"], lw["nope"], lw["kv_rank"], lw["rope"]
    dv = lw["v_head"]
    rgw, xw = lw["w_xrg"].shape[1], lw["w_qx"].shape[1]
    act = F32 if decode else BF16
    tab_rows = cos_t.shape[0]
    n_tab = tab_rows // tm

    def row(width):
        return pl.BlockSpec((tm, width), lambda i: (i, 0))

    tab = pl.BlockSpec((tm, LANES), lambda i: (i % n_tab, 0))
    weights = [lw["w_qlat"], lw["q_norm_g"], lw["w_qb"], lw["w_c"], lw["kv_norm_g"], lw["w_kpe"],
               lw["w_ga"], lw["w_xrg"], lw["w_grg"], lw["w_qx"], lw["w_gx"]]
    if decode:
        weights += [lw["w_ukt"]]
        out_w = [(n_heads * rank, F32), (n_heads * LANES, F32), (rank, F32), (rope, F32), (n_heads * dv, act),
                 (rgw, F32), (rgw, act), (xw, act), (xw, act)]
    else:
        weights += [lw["w_uk"], lw["w_uv"]]
        out_w = [(n_heads * (nope + LANES), BF16), (rank, F32), (rope, F32), (n_heads * nope, BF16),
                 (LANES, BF16), (n_heads * dv, BF16), (n_heads * dv, act), (rgw, F32), (rgw, act), (xw, act),
                 (xw, act)]
    kern = functools.partial(_in_proj_kernel, decode=decode, n_heads=n_heads, nope=nope, half=rope // 2,
                             scale=float((nope + rope) ** -0.5), x_scale=float(lw["x_head_dim"] ** -0.5))
    return pl.pallas_call(
        kern,
        grid=(n // tm,),
        in_specs=[row(d), _full(lw["ln_g"].shape), tab, tab] + [_full(w.shape) for w in weights],
        out_specs=[row(w) for w, _ in out_w],
        out_shape=[jax.ShapeDtypeStruct((n, w), dt) for w, dt in out_w],
        compiler_params=_params(1),
        name="in_proj_dec" if decode else "in_proj",
    )(x, lw["ln_g"], cos_t, sin_t, *weights)


def _mem_kv_kernel(m_ref, g_ref, w_ref, k_ref, v_ref):
    u = _rms(m_ref[...], g_ref[...]).astype(BF16)
    kv = _dot(u, w_ref[...])
    xw = k_ref.shape[1]
    k_ref[...] = kv[:, :xw]
    v_ref[...] = kv[:, xw:]


def _mem_kv(mem, g, w, *, tm):
    n, d = mem.shape
    xw = w.shape[1] // 2
    return pl.pallas_call(
        _mem_kv_kernel,
        grid=(n // tm,),
        in_specs=[pl.BlockSpec((tm, d), lambda i: (i, 0)), _full(g.shape), _full(w.shape)],
        out_specs=[pl.BlockSpec((tm, xw), lambda i: (i, 0))] * 2,
        out_shape=[jax.ShapeDtypeStruct((n, xw), F32)] * 2,
        compiler_params=_params(1),
        name="mem_kv",
    )(mem, g, w)


def _attn_prompt_kernel(q_ref, kn_ref, kp_ref, v_ref, g_ref, o_ref, *, tq):
    qi = pl.program_id(2)
    q = q_ref[...]
    dv = v_ref.shape[1]

    def scores(j):
        off = pl.multiple_of(j * tq, tq)
        k = jnp.concatenate([kn_ref[pl.ds(off, tq), :], kp_ref[pl.ds(off, tq), :]], axis=1)
        return _dot_nt(q, k), v_ref[pl.ds(off, tq), :]

    def update(s, v, carry):
        m, l, acc = carry
        m_new = jnp.maximum(m, jnp.max(s, axis=-1, keepdims=True))
        alpha = jnp.exp(m - m_new)
        p = jnp.exp(s - m_new)
        l = alpha * l + jnp.sum(p, axis=-1, keepdims=True)
        acc = alpha * acc + _dot(p.astype(BF16), v)
        return m_new, l, acc

    def body(j, carry):
        s, v = scores(j)
        return update(s, v, carry)

    init = (jnp.full((tq, 1), NEG_BIG, F32), jnp.zeros((tq, 1), F32), jnp.zeros((tq, dv), F32))
    carry = lax.fori_loop(0, qi, body, init)
    s, v = scores(qi)
    row = lax.broadcasted_iota(jnp.int32, (tq, tq), 0)
    col = lax.broadcasted_iota(jnp.int32, (tq, tq), 1)
    s = jnp.where(col <= row, s, NEG_BIG)
    _, l, acc = update(s, v, carry)
    o_ref[...] = (acc / l * g_ref[...].astype(F32)).astype(o_ref.dtype)


def _attn_prompt(qf, kn, kpb, v, sga, *, batch, seq, n_heads, tq):
    n = qf.shape[0]
    qw = qf.shape[1] // n_heads
    kw = kn.shape[1] // n_heads
    dv = v.shape[1] // n_heads
    nq = seq // tq
    return pl.pallas_call(
        functools.partial(_attn_prompt_kernel, tq=tq),
        grid=(batch, n_heads, nq),
        in_specs=[
            pl.BlockSpec((tq, qw), lambda b, h, i: (b * nq + i, h)),
            pl.BlockSpec((seq, kw), lambda b, h, i: (b, h)),
            pl.BlockSpec((seq, LANES), lambda b, h, i: (b, 0)),
            pl.BlockSpec((seq, dv), lambda b, h, i: (b, h)),
            pl.BlockSpec((tq, dv), lambda b, h, i: (b * nq + i, h)),
        ],
        out_specs=pl.BlockSpec((tq, dv), lambda b, h, i: (b * nq + i, h)),
        out_shape=jax.ShapeDtypeStruct((n, n_heads * dv), BF16),
        compiler_params=_params(3),
        name="attn_prompt",
    )(qf, kn, kpb, v, sga)


def _attn_decode_kernel(pt_ref, qc_ref, qpe_ref, cn_ref, kn_ref, ckv_hbm, kpe_hbm, o_ref,
                        kbuf, pbuf, sem, q_s, qp_s, m_s, l_s, acc_s, *, layer, ch, n_heads):
    b = pl.program_id(0)
    c = pl.program_id(1)
    nb = pl.num_programs(0)
    nc = pl.num_programs(1)
    g = b * nc + c
    slot = g % 2
    page = ckv_hbm.shape[2]
    t_new, rank = cn_ref.shape
    rope = kn_ref.shape[1]

    def copies(bb, cc, sl):
        out = []
        for p in range(ch):
            pg = pt_ref[bb, cc * ch + p]
            out.append(pltpu.make_async_copy(ckv_hbm.at[layer, pg], kbuf.at[sl, pl.ds(p * page, page), :],
                                             sem.at[sl, 0]))
            out.append(pltpu.make_async_copy(kpe_hbm.at[layer, pg], pbuf.at[sl, pl.ds(p * page, page), :],
                                             sem.at[sl, 1]))
        return out

    @pl.when(g == 0)
    def _():
        for cp in copies(b, c, slot):
            cp.start()

    @pl.when(g + 1 < nb * nc)
    def _():
        wrap = c + 1 == nc
        for cp in copies(jnp.where(wrap, b + 1, b), jnp.where(wrap, 0, c + 1), 1 - slot):
            cp.start()

    @pl.when(c == 0)
    def _():
        qc = qc_ref[...]
        qp = qpe_ref[...]
        q_s[...] = jnp.concatenate([qc[:, h * rank:(h + 1) * rank] for h in range(n_heads)], axis=0).astype(BF16)
        qp_s[...] = jnp.concatenate([qp[:, h * LANES:(h + 1) * LANES] for h in range(n_heads)], axis=0).astype(BF16)
        m_s[...] = jnp.full(m_s.shape, NEG_BIG, F32)
        l_s[...] = jnp.zeros(l_s.shape, F32)
        acc_s[...] = jnp.zeros(acc_s.shape, F32)

    for cp in copies(b, c, slot):
        cp.wait()

    q = q_s[...]
    qp = qp_s[...][:, :rope]

    def update(s, kb):
        m = m_s[...]
        m_new = jnp.maximum(m, jnp.max(s, axis=-1, keepdims=True))
        alpha = jnp.exp(m - m_new)
        p = jnp.exp(s - m_new)
        l_s[...] = alpha * l_s[...] + jnp.sum(p, axis=-1, keepdims=True)
        acc_s[...] = alpha * acc_s[...] + _dot(p.astype(BF16), kb)
        m_s[...] = m_new

    kb = kbuf[slot].astype(BF16)
    pb = pbuf[slot].astype(BF16)
    update(_dot_nt(q, kb) + _dot_nt(qp, pb), kb)

    @pl.when(c == nc - 1)
    def _():
        pad = LANES - t_new
        kb2 = jnp.concatenate([cn_ref[...], jnp.zeros((pad, rank), F32)], axis=0).astype(BF16)
        pb2 = jnp.concatenate([kn_ref[...], jnp.zeros((pad, rope), F32)], axis=0).astype(BF16)
        s = _dot_nt(q, kb2) + _dot_nt(qp, pb2)
        rows = s.shape[0]
        row = lax.broadcasted_iota(jnp.int32, (rows, LANES), 0)
        col = lax.broadcasted_iota(jnp.int32, (rows, LANES), 1)
        s = jnp.where(col <= row % t_new, s, NEG_BIG)
        update(s, kb2)
        o = acc_s[...] / l_s[...]
        for h in range(n_heads):
            o_ref[:, h * rank:(h + 1) * rank] = o[h * t_new:(h + 1) * t_new, :]


def _attn_decode(page_table, qc, qpe, c_new, kpe_new, cache_ckv, cache_kpe, *, layer, t_new, n_heads, ch):
    n = qc.shape[0]
    batch = n // t_new
    n_pages = page_table.shape[1]
    page, rank = cache_ckv.shape[2:]
    rope = cache_kpe.shape[3]
    nc = n_pages // ch
    rows = n_heads * t_new

    def per_batch(width):
        return pl.BlockSpec((t_new, width), lambda b, c, pt: (b, 0))

    grid_spec = pltpu.PrefetchScalarGridSpec(
        num_scalar_prefetch=1,
        grid=(batch, nc),
        in_specs=[per_batch(qc.shape[1]), per_batch(qpe.shape[1]), per_batch(rank), per_batch(rope),
                  pl.BlockSpec(memory_space=pl.ANY), pl.BlockSpec(memory_space=pl.ANY)],
        out_specs=per_batch(n_heads * rank),
        scratch_shapes=[
            pltpu.VMEM((2, ch * page, rank), F32),
            pltpu.VMEM((2, ch * page, rope), F32),
            pltpu.SemaphoreType.DMA((2, 2)),
            pltpu.VMEM((rows, rank), BF16),
            pltpu.VMEM((rows, LANES), BF16),
            pltpu.VMEM((rows, 1), F32),
            pltpu.VMEM((rows, 1), F32),
            pltpu.VMEM((rows, rank), F32),
        ],
    )
    return pl.pallas_call(
        functools.partial(_attn_decode_kernel, layer=layer, ch=ch, n_heads=n_heads),
        grid_spec=grid_spec,
        out_shape=jax.ShapeDtypeStruct((n, n_heads * rank), F32),
        compiler_params=_params(2),
        name="attn_decode",
    )(page_table, qc, qpe, c_new, kpe_new, cache_ckv, cache_kpe)


def _expm1_nonpos(y):
    u = jnp.exp(y)
    um1 = u - 1.0
    tiny = um1 == 0.0
    near = um1 * y / jnp.where(tiny, 1.0, jnp.log(jnp.where(tiny, 2.0, u)))
    return jnp.where(y < -0.5, um1, jnp.where(tiny, y, near))


def _rg_gates(xc, wr, br, wi, bi, lam):
    xb = xc.astype(BF16)
    r = _sigmoid(_dot(xb, wr) + br)
    ig = _sigmoid(_dot(xb, wi) + bi)
    z = -lam
    softplus = jnp.maximum(z, 0.0) + jnp.log1p(jnp.exp(-jnp.abs(z)))
    log_a = -RG_C * r * softplus
    a = jnp.exp(log_a)
    bx = jnp.sqrt(-_expm1_nonpos(2.0 * log_a)) * (ig * xc)
    return a, bx


def _rg_prompt_kernel(x_ref, halo_ref, g_ref, cw_ref, cb_ref, wr_ref, br_ref, wi_ref, bi_ref, lam_ref,
                      y_ref, hl_ref, cn_ref, xs, a_s, b_s, h_s, hc_s, *, tc):
    i = pl.program_id(1)
    width = x_ref.shape[1]
    n_tap = cw_ref.shape[0]
    x = x_ref[...]
    xs[SUBLANES:SUBLANES + tc, :] = x

    @pl.when(i == 0)
    def _():
        xs[0:SUBLANES, :] = jnp.zeros((SUBLANES, width), F32)
        hc_s[...] = jnp.zeros(hc_s.shape, F32)

    @pl.when(i > 0)
    def _():
        xs[0:SUBLANES, :] = halo_ref[...]

    xc = cb_ref[...]
    for k in range(n_tap):
        start = SUBLANES - (n_tap - 1) + k
        xc = xc + xs[start:start + tc, :] * cw_ref[k:k + 1, :]
    a, bx = _rg_gates(xc, wr_ref[...], br_ref[...], wi_ref[...], bi_ref[...], lam_ref[...])
    a_s[...] = a
    b_s[...] = bx

    row = lax.broadcasted_iota(jnp.int32, (SUBLANES, width), 0)

    def group(gi, h_prev):
        off = pl.multiple_of(gi * SUBLANES, SUBLANES)
        ag = a_s[pl.ds(off, SUBLANES), :]
        bg = b_s[pl.ds(off, SUBLANES), :]
        d = 1
        while d < SUBLANES:
            keep = row >= d
            bg = bg + ag * jnp.where(keep, pltpu.roll(bg, d, 0), 0.0)
            ag = ag * jnp.where(keep, pltpu.roll(ag, d, 0), 1.0)
            d *= 2
        h = bg + ag * h_prev
        h_s[pl.ds(off, SUBLANES), :] = h
        return jnp.broadcast_to(h[SUBLANES - 1:SUBLANES, :], (SUBLANES, width))

    h_last = lax.fori_loop(0, tc // SUBLANES, group, hc_s[...], unroll=2)
    hc_s[...] = h_last
    y_ref[...] = (h_s[...] * g_ref[...].astype(F32)).astype(y_ref.dtype)
    hl_ref[0] = h_last[0:1, :]
    cn_ref[0] = x[tc - SUBLANES:, :]


def _rg_prompt(x_rg, sg_rg, lw, *, batch, seq, tc):
    n, width = x_rg.shape
    nc = seq // tc
    per8 = tc // SUBLANES
    weights = [lw["conv_w"], lw["conv_b"], lw["w_r"], lw["b_r"], lw["w_i"], lw["b_i"], lw["rg_lambda"]]
    return pl.pallas_call(
        functools.partial(_rg_prompt_kernel, tc=tc),
        grid=(batch, nc),
        in_specs=[
            pl.BlockSpec((tc, width), lambda b, i: (b * nc + i, 0)),
            pl.BlockSpec((SUBLANES, width), lambda b, i: (jnp.maximum((b * nc + i) * per8 - 1, 0), 0)),
            pl.BlockSpec((tc, width), lambda b, i: (b * nc + i, 0)),
        ] + [_full(w.shape) for w in weights],
        out_specs=[
            pl.BlockSpec((tc, width), lambda b, i: (b * nc + i, 0)),
            pl.BlockSpec((1, 1, width), lambda b, i: (b, 0, 0)),
            pl.BlockSpec((1, SUBLANES, width), lambda b, i: (b, 0, 0)),
        ],
        out_shape=[
            jax.ShapeDtypeStruct((n, width), BF16),
            jax.ShapeDtypeStruct((batch, 1, width), F32),
            jax.ShapeDtypeStruct((batch, SUBLANES, width), F32),
        ],
        scratch_shapes=[
            pltpu.VMEM((tc + SUBLANES, width), F32),
            pltpu.VMEM((tc, width), F32),
            pltpu.VMEM((tc, width), F32),
            pltpu.VMEM((tc, width), F32),
            pltpu.VMEM((SUBLANES, width), F32),
        ],
        compiler_params=_params(2),
        name="rg_prompt",
    )(x_rg, x_rg, sg_rg, *weights)


def _rg_decode_kernel(x_ref, g_ref, c0_ref, h0_ref, cw_ref, cb_ref, wr_ref, br_ref, wi_ref, bi_ref, lam_ref,
                      y_ref, hl_ref, cn_ref):
    t_new = x_ref.shape[0]
    n_tap = cw_ref.shape[0]
    xp = [c0_ref[j] for j in range(n_tap - 1)] + [x_ref[t] for t in range(t_new)]
    h = h0_ref[...]
    for t in range(t_new):
        xc = cb_ref[...]
        for k in range(n_tap):
            xc = xc + xp[t + k] * cw_ref[k:k + 1, :]
        a, bx = _rg_gates(xc, wr_ref[...], br_ref[...], wi_ref[...], bi_ref[...], lam_ref[...])
        h = a * h + bx
        y_ref[t] = h * g_ref[t]
    hl_ref[...] = h
    for j in range(n_tap - 1):
        cn_ref[j] = xp[t_new + j]


def _rg_decode(x_tm, g_tm, conv0_tm, h0, lw):
    weights = [lw["conv_w"], lw["conv_b"], lw["w_r"], lw["b_r"], lw["w_i"], lw["b_i"], lw["rg_lambda"]]
    args = [x_tm, g_tm, conv0_tm, h0] + weights
    return pl.pallas_call(
        _rg_decode_kernel,
        in_specs=[_full(a.shape) for a in args],
        out_specs=[_full(x_tm.shape), _full(h0.shape), _full(conv0_tm.shape)],
        out_shape=[jax.ShapeDtypeStruct(x_tm.shape, F32), jax.ShapeDtypeStruct(h0.shape, F32),
                   jax.ShapeDtypeStruct(conv0_tm.shape, F32)],
        grid=(1,),
        compiler_params=_params(1),
        name="rg_decode",
    )(*args)


def _xattn_kernel(q_ref, k_ref, v_ref, g_ref, o_ref, *, n_heads):
    tq = q_ref.shape[0]
    hd = q_ref.shape[1] // n_heads
    min_rows = 2 * SUBLANES
    for h in range(n_heads):
        sl = slice(h * hd, (h + 1) * hd)
        q = q_ref[:, sl]
        if tq < min_rows:
            q = jnp.concatenate([q.astype(F32), jnp.zeros((min_rows - tq, hd), F32)], axis=0)
        k = k_ref[:, sl].astype(BF16)
        v = v_ref[:, sl].astype(BF16)
        s = _dot_nt(q.astype(BF16), k)
        p = jnp.exp(s - jnp.max(s, axis=-1, keepdims=True))
        o = _dot(p.astype(BF16), v) / jnp.sum(p, axis=-1, keepdims=True)
        o_ref[:, sl] = (o[:tq] * g_ref[:, sl].astype(F32)).astype(o_ref.dtype)


def _xattn(qx, mem_k, mem_v, sgx, *, batch, seq, n_mem, n_heads, tq, base=0):
    n, xw = qx.shape
    nq = seq // tq
    return pl.pallas_call(
        functools.partial(_xattn_kernel, n_heads=n_heads),
        grid=(batch, nq),
        in_specs=[
            pl.BlockSpec((tq, xw), lambda b, i: (b * nq + i, 0)),
            pl.BlockSpec((n_mem, xw), lambda b, i: (base + b, 0)),
            pl.BlockSpec((n_mem, xw), lambda b, i: (base + b, 0)),
            pl.BlockSpec((tq, xw), lambda b, i: (b * nq + i, 0)),
        ],
        out_specs=pl.BlockSpec((tq, xw), lambda b, i: (b * nq + i, 0)),
        out_shape=jax.ShapeDtypeStruct((n, xw), qx.dtype),
        compiler_params=_params(2),
        name="xattn",
    )(qx, mem_k, mem_v, sgx)


def _out_proj_kernel(*refs, decode, final, n_heads):
    if decode:
        ol_ref, sga_ref, wuv_ref = refs[:3]
        refs = refs[3:]
    else:
        ya_ref = refs[0]
        refs = refs[1:]
    yb_ref, yc_ref, x_ref, woa_ref, wob_ref, woc_ref = refs[:6]
    refs = refs[6:]
    if final:
        fg_ref, xo_ref, yo_ref = refs
    else:
        (xo_ref,) = refs

    acc = x_ref[...] + _dot(yb_ref[...].astype(BF16), wob_ref[...]) + _dot(yc_ref[...].astype(BF16), woc_ref[...])
    if decode:
        rank, dv = wuv_ref.shape[1:]
        for h in range(n_heads):
            o_a = _dot(ol_ref[:, h * rank:(h + 1) * rank].astype(BF16), wuv_ref[h])
            y_a = (o_a * sga_ref[:, h * dv:(h + 1) * dv]).astype(BF16)
            acc = acc + _dot(y_a, woa_ref[h * dv:(h + 1) * dv, :])
    else:
        acc = acc + _dot(ya_ref[...], woa_ref[...])
    xo_ref[...] = acc
    if final:
        yo_ref[...] = _rms(acc, fg_ref[...])


def _out_proj(first, y_b, y_c, x, lw, final_g, *, decode, tm):
    n, d = x.shape
    weights = [lw["w_oa"], lw["w_ob"], lw["w_oc"]]
    final = final_g is not None
    acts = list(first) + [y_b, y_c, x]
    if decode:
        in_specs = [pl.BlockSpec((tm, first[0].shape[1]), lambda i: (i, 0)),
                    pl.BlockSpec((tm, first[1].shape[1]), lambda i: (i, 0)), _full(first[2].shape)]
    else:
        in_specs = [pl.BlockSpec((tm, first[0].shape[1]), lambda i: (i, 0))]
    in_specs += [pl.BlockSpec((tm, a.shape[1]), lambda i: (i, 0)) for a in (y_b, y_c, x)]
    in_specs += [_full(w.shape) for w in weights]
    args = acts + weights
    n_out = 1
    if final:
        in_specs.append(_full(final_g.shape))
        args.append(final_g)
        n_out = 2
    out = pl.pallas_call(
        functools.partial(_out_proj_kernel, decode=decode, final=final, n_heads=lw["n_heads"]),
        grid=(n // tm,),
        in_specs=in_specs,
        out_specs=[pl.BlockSpec((tm, d), lambda i: (i, 0))] * n_out,
        out_shape=[jax.ShapeDtypeStruct((n, d), F32)] * n_out,
        compiler_params=_params(1),
        name="out_proj_dec" if decode else "out_proj",
    )(*args)
    return out if final else (out[0], None)


def _rope_tables(pos, rope, rows):
    half = rope // 2
    inv = 1.0 / (ROPE_BASE ** (jnp.arange(half, dtype=F32) / half))
    ang = pos.astype(F32)[:, None] * inv[None, :]
    cos, sin = jnp.cos(ang), jnp.sin(ang)
    zeros = jnp.zeros((pos.shape[0], LANES - rope), F32)
    cos_t = jnp.concatenate([cos, cos, zeros], axis=1)
    sin_t = jnp.concatenate([-sin, sin, zeros], axis=1)
    reps = max(rows // pos.shape[0], 1)
    return jnp.tile(cos_t, (reps, 1)), jnp.tile(sin_t, (reps, 1))


def _block_diag(w):
    nb, bs, _ = w.shape
    eye = jnp.eye(nb, dtype=w.dtype)
    return (w[:, :, None, :] * eye[:, None, :, None]).reshape(nb * bs, nb * bs)


def _layer_weights(l, ln_g, w_in, q_norm_g, kv_norm_g, w_q_b, w_kv_b, conv_w, conv_b, w_r, b_r, w_i, b_i,
                   rg_lambda, w_out, rope, x_heads):
    d = w_in.shape[1]
    q_rank, n_heads, qk = w_q_b.shape[1:]
    kv_rank = w_kv_b.shape[1]
    nope = qk - rope
    dv = w_kv_b.shape[3] - nope
    rgw = conv_w.shape[2]
    mla_w = n_heads * dv
    xw = (w_in.shape[2] - q_rank - kv_rank - rope - mla_w - 2 * rgw) // 2
    splits, off = [], 0
    for width in (q_rank, kv_rank, rope, mla_w, rgw, rgw, xw, xw):
        splits.append(w_in[l][:, off:off + width].astype(BF16))
        off += width
    w_qlat, w_c, w_kpe, w_ga, w_xrg, w_grg, w_qx, w_gx = splits
    w_qb = jnp.pad(w_q_b[l], ((0, 0), (0, 0), (0, LANES - rope))).reshape(q_rank, n_heads * (nope + LANES))
    wo = w_out[l].astype(BF16)
    return dict(
        n_heads=n_heads, nope=nope, rope=rope, kv_rank=kv_rank, v_head=dv, x_head_dim=xw // x_heads,
        ln_g=ln_g[l][None, :], w_qlat=w_qlat, q_norm_g=q_norm_g[l][None, :], w_qb=w_qb.astype(BF16),
        w_c=w_c, kv_norm_g=kv_norm_g[l][None, :], w_kpe=jnp.pad(w_kpe, ((0, 0), (0, LANES - rope))),
        w_ga=w_ga, w_xrg=w_xrg, w_grg=w_grg, w_qx=w_qx, w_gx=w_gx,
        w_uk=w_kv_b[l][:, :, :nope].reshape(kv_rank, n_heads * nope).astype(BF16),
        w_uv=w_kv_b[l][:, :, nope:].reshape(kv_rank, n_heads * dv).astype(BF16),
        w_ukt=jnp.transpose(w_kv_b[l][:, :, :nope], (1, 2, 0)).astype(BF16),
        w_uvh=jnp.transpose(w_kv_b[l][:, :, nope:], (1, 0, 2)).astype(BF16),
        conv_w=conv_w[l], conv_b=conv_b[l][None, :],
        w_r=_block_diag(w_r[l]).astype(BF16), b_r=b_r[l].reshape(1, rgw),
        w_i=_block_diag(w_i[l]).astype(BF16), b_i=b_i[l].reshape(1, rgw),
        rg_lambda=rg_lambda[l][None, :],
        w_oa=wo[:mla_w], w_ob=wo[mla_w:mla_w + rgw], w_oc=wo[mla_w + rgw:],
    )


def _pick(n, pref):
    t = min(n, pref)
    while n % t:
        t //= 2
    return t


def kernel(x_prompt, x_sample, mem_prompt, cache_ckv, cache_kpe, state_rg_h, state_rg_conv, cache_mem_k, cache_mem_v, page_table, ln_g, w_in, q_norm_g, kv_norm_g, w_q_b, w_kv_b, conv_w, conv_b, w_r, b_r, w_i, b_i, rg_lambda, mem_norm_g, w_mem_kv, w_out, final_norm_g):
    batch, seq, d = x_prompt.shape
    bd, t_new, _ = x_sample.shape
    depth = w_in.shape[0]
    n_mem = mem_prompt.shape[1]
    x_heads, x_hd = cache_mem_k.shape[3:]
    n_pages = page_table.shape[1]
    page = cache_ckv.shape[2]
    rope = cache_kpe.shape[3]
    n_tap = conv_w.shape[1]
    assert t_new >= n_tap - 1 and seq >= SUBLANES and t_new == SUBLANES

    tm_p = _pick(seq, 512)
    tm_d = _pick(bd * t_new, 256)
    tq = _pick(seq, 256)
    tc = _pick(seq, 256)
    ch = _pick(n_pages, 16)

    cos_p, sin_p = _rope_tables(jnp.arange(seq, dtype=jnp.int32), rope, tm_p)
    cos_d, sin_d = _rope_tables(n_pages * page + jnp.arange(t_new, dtype=jnp.int32), rope, tm_d)

    xp = x_prompt.reshape(batch * seq, d)
    xs = x_sample.reshape(bd * t_new, d)
    mem = mem_prompt.reshape(batch * n_mem, d)
    fg = final_norm_g[None, :]
    cmk = cache_mem_k.reshape(depth * bd * n_mem, x_heads * x_hd)
    cmv = cache_mem_v.reshape(depth * bd * n_mem, x_heads * x_hd)

    outs = [[] for _ in range(10)]
    yp = ys = None
    for l in range(depth):
        lw = _layer_weights(l, ln_g, w_in, q_norm_g, kv_norm_g, w_q_b, w_kv_b, conv_w, conv_b, w_r, b_r, w_i, b_i,
                            rg_lambda, w_out, rope, x_heads)
        n_heads = lw["n_heads"]
        last = l == depth - 1

        mk, mv = _mem_kv(mem, mem_norm_g[l][None, :], w_mem_kv[l].astype(BF16), tm=_pick(batch * n_mem, 512))
        qf, c_p, kpe_p, kn, kpb, v, sga, xrg, sgr, qx, sgx = _in_proj(xp, lw, cos_p, sin_p, decode=False, tm=tm_p,
                                                                      seq=seq)
        y_a = _attn_prompt(qf, kn, kpb, v, sga, batch=batch, seq=seq, n_heads=n_heads, tq=tq)
        y_b, h_p, cv_p = _rg_prompt(xrg, sgr, lw, batch=batch, seq=seq, tc=tc)
        y_c = _xattn(qx, mk, mv, sgx, batch=batch, seq=seq, n_mem=n_mem, n_heads=x_heads, tq=_pick(seq, 512))
        xp, yp = _out_proj([y_a], y_b, y_c, xp, lw, fg if last else None, decode=False, tm=tm_p)

        qc, qpe, c_s, kpe_s, sga, xrg, sgr, qx, sgx = _in_proj(xs, lw, cos_d, sin_d, decode=True, tm=tm_d, seq=t_new)
        o_lat = _attn_decode(page_table, qc, qpe, c_s, kpe_s, cache_ckv, cache_kpe, layer=l, t_new=t_new,
                             n_heads=n_heads, ch=ch)
        rgw = xrg.shape[1]
        to_tm = lambda a: jnp.transpose(a.reshape(bd, -1, rgw), (1, 0, 2))
        y_tm, h_s, cv_tm = _rg_decode(to_tm(xrg), to_tm(sgr), jnp.transpose(state_rg_conv[l], (1, 0, 2)),
                                      state_rg_h[l], lw)
        y_b = jnp.transpose(y_tm, (1, 0, 2)).reshape(bd * t_new, rgw)
        y_c = _xattn(qx, cmk, cmv, sgx, batch=bd, seq=t_new, n_mem=n_mem, n_heads=x_heads, tq=t_new, base=l * bd)
        xs, ys = _out_proj([o_lat, sga, lw["w_uvh"]], y_b, y_c, xs, lw, fg if last else None, decode=True, tm=tm_d)

        for lst, val in zip(outs, (
                c_p.reshape(batch, seq, -1), kpe_p.reshape(batch, seq, -1), h_p[:, 0], cv_p[:, SUBLANES - (n_tap - 1):],
                mk.reshape(batch, n_mem, x_heads, x_hd), mv.reshape(batch, n_mem, x_heads, x_hd),
                c_s.reshape(bd, t_new, -1), kpe_s.reshape(bd, t_new, -1), h_s, jnp.transpose(cv_tm, (1, 0, 2)))):
            lst.append(val)

    return (yp.reshape(batch, seq, d), ys.reshape(bd, t_new, d)) + tuple(jnp.stack(o) for o in outs)
```

```python
import functools

import jax
import jax.numpy as jnp
from jax import lax
from jax.experimental import pallas as pl
from jax.experimental.pallas import tpu as pltpu

EPS = 1e-6
ROPE_BASE = 10000.0
RG_C = 8.0
LANES = 128
SUBLANES = 8
NEG_BIG = -1e30
VMEM_LIMIT = 56 * 1024 * 1024

F32 = jnp.float32
BF16 = jnp.bfloat16

_NT = (((1,), (1,)), ((), ()))


def _params(n_grid, vmem=VMEM_LIMIT):
    return pltpu.CompilerParams(dimension_semantics=("arbitrary",) * n_grid, vmem_limit_bytes=vmem)


def _full(shape):
    n = len(shape)
    return pl.BlockSpec(shape, lambda *_: (0,) * n)


def _rms(x, g):
    return x * lax.rsqrt(jnp.mean(x * x, axis=-1, keepdims=True) + EPS) * g


def _sigmoid(x):
    return 1.0 / (1.0 + jnp.exp(-x))


def _silu(x):
    return x * _sigmoid(x)


def _dot(a, b):
    return jnp.dot(a, b, preferred_element_type=F32)


def _dot_nt(a, b):
    return lax.dot_general(a, b, _NT, preferred_element_type=F32)


def _rope_padded(x, cos_t, sin_t, half):
    swapped = pltpu.roll(x, half, 1) + pltpu.roll(x, LANES - half, 1)
    return x * cos_t + swapped * sin_t


def _in_proj_kernel(*refs, decode, n_heads, nope, half, scale, x_scale):
    (x_ref, lng_ref, cos_ref, sin_ref, wq_ref, qg_ref, wqb_ref, wc_ref, cg_ref, wk_ref,
     wa_ref, wxr_ref, wgr_ref, wqx_ref, wgx_ref) = refs[:15]
    if decode:
        wukt_ref = refs[15]
        (qc_ref, qpe_ref, c_ref, kpe_ref, sga_ref, xrg_ref, sgr_ref, qx_ref, sgx_ref) = refs[16:]
    else:
        wuk_ref, wuv_ref = refs[15:17]
        (qf_ref, c_ref, kpe_ref, kn_ref, kpb_ref, v_ref, sga_ref, xrg_ref, sgr_ref, qx_ref,
         sgx_ref) = refs[17:]

    u = _rms(x_ref[...], lng_ref[...]).astype(BF16)
    cos_t = cos_ref[...]
    sin_t = sin_ref[...]

    qn = _rms(_dot(u, wq_ref[...]), qg_ref[...]).astype(BF16)
    hw = nope + LANES
    for h in range(n_heads):
        qh = _dot(qn, wqb_ref[:, h * hw:(h + 1) * hw])
        q_nope = qh[:, :nope] * scale
        q_pe = _rope_padded(qh[:, nope:], cos_t, sin_t, half) * scale
        if decode:
            qc_ref[:, h * wukt_ref.shape[2]:(h + 1) * wukt_ref.shape[2]] = _dot(q_nope.astype(BF16), wukt_ref[h])
            qpe_ref[:, h * LANES:(h + 1) * LANES] = q_pe
        else:
            qf_ref[:, h * hw:h * hw + nope] = q_nope.astype(BF16)
            qf_ref[:, h * hw + nope:(h + 1) * hw] = q_pe.astype(BF16)

    c_new = _rms(_dot(u, wc_ref[...]), cg_ref[...])
    c_ref[...] = c_new
    kpe = _rope_padded(_dot(u, wk_ref[...]), cos_t, sin_t, half)
    kpe_ref[...] = kpe[:, :kpe_ref.shape[1]]
    if not decode:
        cb = c_new.astype(BF16)
        kn_ref[...] = _dot(cb, wuk_ref[...]).astype(BF16)
        v_ref[...] = _dot(cb, wuv_ref[...]).astype(BF16)
        kpb_ref[...] = kpe.astype(BF16)

    sga_ref[...] = _silu(_dot(u, wa_ref[...])).astype(sga_ref.dtype)
    xrg_ref[...] = _dot(u, wxr_ref[...])
    sgr_ref[...] = _silu(_dot(u, wgr_ref[...])).astype(sgr_ref.dtype)
    qx_ref[...] = (_dot(u, wqx_ref[...]) * x_scale).astype(qx_ref.dtype)
    sgx_ref[...] = _silu(_dot(u, wgx_ref[...])).astype(sgx_ref.dtype)


def _in_proj(x, lw, cos_t, sin_t, *, decode, tm, seq):
    n, d = x.shape
    n_heads, nope, rank, rope = lw["n_heads"], lw["nope"], lw["kv_rank"], lw["rope"]
    dv = lw["v_head"]
    rgw, xw = lw["w_xrg"].shape[1], lw["w_qx"].shape[1]
    act = F32 if decode else BF16
    tab_rows = cos_t.shape[0]
    n_tab = tab_rows // tm

    def row(width):
        return pl.BlockSpec((tm, width), lambda i: (i, 0))

    tab = pl.BlockSpec((tm, LANES), lambda i: (i % n_tab, 0))
    weights = [lw["w_qlat"], lw["q_norm_g"], lw["w_qb"], lw["w_c"], lw["kv_norm_g"], lw["w_kpe"],
               lw["w_ga"], lw["w_xrg"], lw["w_grg"], lw["w_qx"], lw["w_gx"]]
    if decode:
        weights += [lw["w_ukt"]]
        out_w = [(n_heads * rank, F32), (n_heads * LANES, F32), (rank, F32), (rope, F32), (n_heads * dv, act),
                 (rgw, F32), (rgw, act), (xw, act), (xw, act)]
    else:
        weights += [lw["w_uk"], lw["w_uv"]]
        out_w = [(n_heads * (nope + LANES), BF16), (rank, F32), (rope, F32), (n_heads * nope, BF16),
                 (LANES, BF16), (n_heads * dv, BF16), (n_heads * dv, act), (rgw, F32), (rgw, act), (xw, act),
                 (xw, act)]
    kern = functools.partial(_in_proj_kernel, decode=decode, n_heads=n_heads, nope=nope, half=rope // 2,
                             scale=float((nope + rope) ** -0.5), x_scale=float(lw["x_head_dim"] ** -0.5))
    return pl.pallas_call(
        kern,
        grid=(n // tm,),
        in_specs=[row(d), _full(lw["ln_g"].shape), tab, tab] + [_full(w.shape) for w in weights],
        out_specs=[row(w) for w, _ in out_w],
        out_shape=[jax.ShapeDtypeStruct((n, w), dt) for w, dt in out_w],
        compiler_params=_params(1),
        name="in_proj_dec" if decode else "in_proj",
    )(x, lw["ln_g"], cos_t, sin_t, *weights)


def _mem_kv_kernel(m_ref, g_ref, w_ref, k_ref, v_ref):
    u = _rms(m_ref[...], g_ref[...]).astype(BF16)
    kv = _dot(u, w_ref[...])
    xw = k_ref.shape[1]
    k_ref[...] = kv[:, :xw]
    v_ref[...] = kv[:, xw:]


def _mem_kv(mem, g, w, *, tm):
    n, d = mem.shape
    xw = w.shape[1] // 2
    return pl.pallas_call(
        _mem_kv_kernel,
        grid=(n // tm,),
        in_specs=[pl.BlockSpec((tm, d), lambda i: (i, 0)), _full(g.shape), _full(w.shape)],
        out_specs=[pl.BlockSpec((tm, xw), lambda i: (i, 0))] * 2,
        out_shape=[jax.ShapeDtypeStruct((n, xw), F32)] * 2,
        compiler_params=_params(1),
        name="mem_kv",
    )(mem, g, w)


def _attn_prompt_kernel(q_ref, kn_ref, kp_ref, v_ref, g_ref, o_ref, *, tq, hp):
    qi = pl.program_id(2)
    qw = q_ref.shape[1] // hp
    kw = kn_ref.shape[1] // hp
    dv = v_ref.shape[1] // hp

    def scores(j, h):
        off = pl.multiple_of(j * tq, tq)
        k = jnp.concatenate([kn_ref[pl.ds(off, tq), h * kw:(h + 1) * kw], kp_ref[pl.ds(off, tq), :]], axis=1)
        return _dot_nt(q_ref[:, h * qw:(h + 1) * qw], k), v_ref[pl.ds(off, tq), h * dv:(h + 1) * dv]

    def update(s, v, carry):
        m, l, acc = carry
        m_new = jnp.maximum(m, jnp.max(s, axis=-1, keepdims=True))
        alpha = jnp.exp(m - m_new)
        p = jnp.exp(s - m_new)
        l = alpha * l + jnp.sum(p, axis=-1, keepdims=True)
        acc = alpha * acc + _dot(p.astype(BF16), v)
        return m_new, l, acc

    def body(j, carry):
        return tuple(update(*scores(j, h), carry[h]) for h in range(hp))

    init = tuple((jnp.full((tq, 1), NEG_BIG, F32), jnp.zeros((tq, 1), F32), jnp.zeros((tq, dv), F32))
                 for _ in range(hp))
    carry = lax.fori_loop(0, qi, body, init)
    row = lax.broadcasted_iota(jnp.int32, (tq, tq), 0)
    col = lax.broadcasted_iota(jnp.int32, (tq, tq), 1)
    for h in range(hp):
        s, v = scores(qi, h)
        _, l, acc = update(jnp.where(col <= row, s, NEG_BIG), v, carry[h])
        sl = slice(h * dv, (h + 1) * dv)
        o_ref[:, sl] = (acc / l * g_ref[:, sl].astype(F32)).astype(o_ref.dtype)


def _attn_prompt(qf, kn, kpb, v, sga, *, batch, seq, n_heads, tq, hp):
    n = qf.shape[0]
    qw = qf.shape[1] // n_heads * hp
    kw = kn.shape[1] // n_heads * hp
    dv = v.shape[1] // n_heads * hp
    nq = seq // tq
    return pl.pallas_call(
        functools.partial(_attn_prompt_kernel, tq=tq, hp=hp),
        grid=(batch, n_heads // hp, nq),
        in_specs=[
            pl.BlockSpec((tq, qw), lambda b, h, i: (b * nq + i, h)),
            pl.BlockSpec((seq, kw), lambda b, h, i: (b, h)),
            pl.BlockSpec((seq, LANES), lambda b, h, i: (b, 0)),
            pl.BlockSpec((seq, dv), lambda b, h, i: (b, h)),
            pl.BlockSpec((tq, dv), lambda b, h, i: (b * nq + i, h)),
        ],
        out_specs=pl.BlockSpec((tq, dv), lambda b, h, i: (b * nq + i, h)),
        out_shape=jax.ShapeDtypeStruct((n, v.shape[1]), BF16),
        compiler_params=_params(3),
        name="attn_prompt",
    )(qf, kn, kpb, v, sga)


def _attn_decode_kernel(pt_ref, qc_ref, qpe_ref, cn_ref, kn_ref, ckv_hbm, kpe_hbm, o_ref,
                        kbuf, pbuf, sem, q_s, qp_s, m_s, l_s, acc_s, *, layer, ch, n_heads):
    b = pl.program_id(0)
    c = pl.program_id(1)
    nb = pl.num_programs(0)
    nc = pl.num_programs(1)
    g = b * nc + c
    slot = g % 2
    page = ckv_hbm.shape[2]
    t_new, rank = cn_ref.shape
    rope = kn_ref.shape[1]

    def copies(bb, cc, sl):
        out = []
        for p in range(ch):
            pg = pt_ref[bb, cc * ch + p]
            out.append(pltpu.make_async_copy(ckv_hbm.at[layer, pg], kbuf.at[sl, pl.ds(p * page, page), :],
                                             sem.at[sl, 0]))
            out.append(pltpu.make_async_copy(kpe_hbm.at[layer, pg], pbuf.at[sl, :, pl.ds(p * page, page)],
                                             sem.at[sl, 1]))
        return out

    @pl.when(g == 0)
    def _():
        for cp in copies(b, c, slot):
            cp.start()

    @pl.when(g + 1 < nb * nc)
    def _():
        wrap = c + 1 == nc
        for cp in copies(jnp.where(wrap, b + 1, b), jnp.where(wrap, 0, c + 1), 1 - slot):
            cp.start()

    @pl.when(c == 0)
    def _():
        qc = qc_ref[...]
        qp = qpe_ref[...]
        q_s[...] = jnp.concatenate([qc[:, h * rank:(h + 1) * rank] for h in range(n_heads)], axis=0).astype(BF16)
        qp_s[...] = jnp.concatenate([qp[:, h * LANES:(h + 1) * LANES] for h in range(n_heads)], axis=0).astype(BF16)
        m_s[...] = jnp.full(m_s.shape, NEG_BIG, F32)
        l_s[...] = jnp.zeros(l_s.shape, F32)
        acc_s[...] = jnp.zeros(acc_s.shape, F32)

    for cp in copies(b, c, slot):
        cp.wait()

    q = q_s[...]
    qp = qp_s[...][:, :rope]

    def update(s, kb):
        m = m_s[...]
        m_new = jnp.maximum(m, jnp.max(s, axis=-1, keepdims=True))
        alpha = jnp.exp(m - m_new)
        p = jnp.exp(s - m_new)
        l_s[...] = alpha * l_s[...] + jnp.sum(p, axis=-1, keepdims=True)
        acc_s[...] = alpha * acc_s[...] + _dot(p.astype(BF16), kb)
        m_s[...] = m_new

    kb = kbuf[slot].astype(BF16)
    pb = pbuf[slot].astype(BF16)
    update(_dot_nt(q, kb) + _dot(qp, pb), kb)

    @pl.when(c == nc - 1)
    def _():
        pad = LANES - t_new
        kb2 = jnp.concatenate([cn_ref[...], jnp.zeros((pad, rank), F32)], axis=0).astype(BF16)
        pb2 = jnp.concatenate([kn_ref[...], jnp.zeros((pad, rope), F32)], axis=0).astype(BF16)
        s = _dot_nt(q, kb2) + _dot_nt(qp, pb2)
        rows = s.shape[0]
        row = lax.broadcasted_iota(jnp.int32, (rows, LANES), 0)
        col = lax.broadcasted_iota(jnp.int32, (rows, LANES), 1)
        s = jnp.where(col <= row % t_new, s, NEG_BIG)
        update(s, kb2)
        o = acc_s[...] / l_s[...]
        for h in range(n_heads):
            o_ref[:, h * rank:(h + 1) * rank] = o[h * t_new:(h + 1) * t_new, :]


def _attn_decode(page_table, qc, qpe, c_new, kpe_new, cache_ckv, cache_kpe_t, *, layer, t_new, n_heads, ch):
    n = qc.shape[0]
    batch = n // t_new
    n_pages = page_table.shape[1]
    page, rank = cache_ckv.shape[2:]
    rope = cache_kpe_t.shape[2]
    nc = n_pages // ch
    rows = n_heads * t_new

    def per_batch(width):
        return pl.BlockSpec((t_new, width), lambda b, c, pt: (b, 0))

    grid_spec = pltpu.PrefetchScalarGridSpec(
        num_scalar_prefetch=1,
        grid=(batch, nc),
        in_specs=[per_batch(qc.shape[1]), per_batch(qpe.shape[1]), per_batch(rank), per_batch(rope),
                  pl.BlockSpec(memory_space=pl.ANY), pl.BlockSpec(memory_space=pl.ANY)],
        out_specs=per_batch(n_heads * rank),
        scratch_shapes=[
            pltpu.VMEM((2, ch * page, rank), F32),
            pltpu.VMEM((2, rope, ch * page), F32),
            pltpu.SemaphoreType.DMA((2, 2)),
            pltpu.VMEM((rows, rank), BF16),
            pltpu.VMEM((rows, LANES), BF16),
            pltpu.VMEM((rows, 1), F32),
            pltpu.VMEM((rows, 1), F32),
            pltpu.VMEM((rows, rank), F32),
        ],
    )
    return pl.pallas_call(
        functools.partial(_attn_decode_kernel, layer=layer, ch=ch, n_heads=n_heads),
        grid_spec=grid_spec,
        out_shape=jax.ShapeDtypeStruct((n, n_heads * rank), F32),
        compiler_params=_params(2),
        name="attn_decode",
    )(page_table, qc, qpe, c_new, kpe_new, cache_ckv, cache_kpe_t)


def _rg_gates(xc, wr, br, wi, bi, lam):
    xb = xc.astype(BF16)
    r = _sigmoid(_dot(xb, wr) + br)
    ig = _sigmoid(_dot(xb, wi) + bi)
    z = -lam
    softplus = jnp.maximum(z, 0.0) + jnp.log1p(jnp.exp(-jnp.abs(z)))
    log_a = -RG_C * r * softplus
    a = jnp.exp(log_a)
    bx = jnp.sqrt(jnp.tanh(-log_a) * (1.0 + a * a)) * (ig * xc)
    return a, bx


def _rg_prompt_kernel(x_ref, halo_ref, g_ref, cw_ref, cb_ref, wr_ref, br_ref, wi_ref, bi_ref, lam_ref,
                      y_ref, hl_ref, cn_ref, xs, a_s, b_s, h_s, hc_s, *, tc):
    i = pl.program_id(1)
    width = x_ref.shape[1]
    n_tap = cw_ref.shape[0]
    x = x_ref[...]
    xs[SUBLANES:SUBLANES + tc, :] = x

    @pl.when(i == 0)
    def _():
        xs[0:SUBLANES, :] = jnp.zeros((SUBLANES, width), F32)
        hc_s[...] = jnp.zeros(hc_s.shape, F32)

    @pl.when(i > 0)
    def _():
        xs[0:SUBLANES, :] = halo_ref[...]

    xc = cb_ref[...]
    for k in range(n_tap):
        start = SUBLANES - (n_tap - 1) + k
        xc = xc + xs[start:start + tc, :] * cw_ref[k:k + 1, :]
    a, bx = _rg_gates(xc, wr_ref[...], br_ref[...], wi_ref[...], bi_ref[...], lam_ref[...])
    a_s[...] = a
    b_s[...] = bx

    row = lax.broadcasted_iota(jnp.int32, (SUBLANES, width), 0)

    def group(gi, h_prev):
        off = pl.multiple_of(gi * SUBLANES, SUBLANES)
        ag = a_s[pl.ds(off, SUBLANES), :]
        bg = b_s[pl.ds(off, SUBLANES), :]
        d = 1
        while d < SUBLANES:
            keep = row >= d
            bg = bg + ag * jnp.where(keep, pltpu.roll(bg, d, 0), 0.0)
            ag = ag * jnp.where(keep, pltpu.roll(ag, d, 0), 1.0)
            d *= 2
        h = bg + ag * h_prev
        h_s[pl.ds(off, SUBLANES), :] = h
        return jnp.broadcast_to(h[SUBLANES - 1:SUBLANES, :], (SUBLANES, width))

    h_last = lax.fori_loop(0, tc // SUBLANES, group, hc_s[...], unroll=2)
    hc_s[...] = h_last
    y_ref[...] = (h_s[...] * g_ref[...].astype(F32)).astype(y_ref.dtype)
    hl_ref[0] = h_last[0:1, :]
    cn_ref[0] = x[tc - SUBLANES:, :]


def _rg_prompt(x_rg, sg_rg, lw, *, batch, seq, tc):
    n, width = x_rg.shape
    nc = seq // tc
    per8 = tc // SUBLANES
    weights = [lw["conv_w"], lw["conv_b"], lw["w_r"], lw["b_r"], lw["w_i"], lw["b_i"], lw["rg_lambda"]]
    return pl.pallas_call(
        functools.partial(_rg_prompt_kernel, tc=tc),
        grid=(batch, nc),
        in_specs=[
            pl.BlockSpec((tc, width), lambda b, i: (b * nc + i, 0)),
            pl.BlockSpec((SUBLANES, width), lambda b, i: (jnp.maximum((b * nc + i) * per8 - 1, 0), 0)),
            pl.BlockSpec((tc, width), lambda b, i: (b * nc + i, 0)),
        ] + [_full(w.shape) for w in weights],
        out_specs=[
            pl.BlockSpec((tc, width), lambda b, i: (b * nc + i, 0)),
            pl.BlockSpec((1, 1, width), lambda b, i: (b, 0, 0)),
            pl.BlockSpec((1, SUBLANES, width), lambda b, i: (b, 0, 0)),
        ],
        out_shape=[
            jax.ShapeDtypeStruct((n, width), BF16),
            jax.ShapeDtypeStruct((batch, 1, width), F32),
            jax.ShapeDtypeStruct((batch, SUBLANES, width), F32),
        ],
        scratch_shapes=[
            pltpu.VMEM((tc + SUBLANES, width), F32),
            pltpu.VMEM((tc, width), F32),
            pltpu.VMEM((tc, width), F32),
            pltpu.VMEM((tc, width), F32),
            pltpu.VMEM((SUBLANES, width), F32),
        ],
        compiler_params=_params(2),
        name="rg_prompt",
    )(x_rg, x_rg, sg_rg, *weights)


def _rg_decode_kernel(x_ref, g_ref, c0_ref, h0_ref, cw_ref, cb_ref, wr_ref, br_ref, wi_ref, bi_ref, lam_ref,
                      y_ref, hl_ref, cn_ref):
    t_new = x_ref.shape[0]
    n_tap = cw_ref.shape[0]
    xp = [c0_ref[j] for j in range(n_tap - 1)] + [x_ref[t] for t in range(t_new)]
    h = h0_ref[...]
    for t in range(t_new):
        xc = cb_ref[...]
        for k in range(n_tap):
            xc = xc + xp[t + k] * cw_ref[k:k + 1, :]
        a, bx = _rg_gates(xc, wr_ref[...], br_ref[...], wi_ref[...], bi_ref[...], lam_ref[...])
        h = a * h + bx
        y_ref[t] = h * g_ref[t]
    hl_ref[...] = h
    for j in range(n_tap - 1):
        cn_ref[j] = xp[t_new + j]


def _rg_decode(x_tm, g_tm, conv0_tm, h0, lw):
    weights = [lw["conv_w"], lw["conv_b"], lw["w_r"], lw["b_r"], lw["w_i"], lw["b_i"], lw["rg_lambda"]]
    args = [x_tm, g_tm, conv0_tm, h0] + weights
    return pl.pallas_call(
        _rg_decode_kernel,
        in_specs=[_full(a.shape) for a in args],
        out_specs=[_full(x_tm.shape), _full(h0.shape), _full(conv0_tm.shape)],
        out_shape=[jax.ShapeDtypeStruct(x_tm.shape, F32), jax.ShapeDtypeStruct(h0.shape, F32),
                   jax.ShapeDtypeStruct(conv0_tm.shape, F32)],
        grid=(1,),
        compiler_params=_params(1),
        name="rg_decode",
    )(*args)


def _xattn_kernel(q_ref, k_ref, v_ref, g_ref, o_ref, *, n_heads):
    tq = q_ref.shape[0]
    hd = q_ref.shape[1] // n_heads
    min_rows = 2 * SUBLANES
    for h in range(n_heads):
        sl = slice(h * hd, (h + 1) * hd)
        q = q_ref[:, sl]
        if tq < min_rows:
            q = jnp.concatenate([q.astype(F32), jnp.zeros((min_rows - tq, hd), F32)], axis=0)
        k = k_ref[:, sl].astype(BF16)
        v = v_ref[:, sl].astype(BF16)
        s = _dot_nt(q.astype(BF16), k)
        p = jnp.exp(s - jnp.max(s, axis=-1, keepdims=True))
        o = _dot(p.astype(BF16), v) / jnp.sum(p, axis=-1, keepdims=True)
        o_ref[:, sl] = (o[:tq] * g_ref[:, sl].astype(F32)).astype(o_ref.dtype)


def _xattn(qx, mem_k, mem_v, sgx, *, batch, seq, n_mem, n_heads, tq, base=0):
    n, xw = qx.shape
    nq = seq // tq
    return pl.pallas_call(
        functools.partial(_xattn_kernel, n_heads=n_heads),
        grid=(batch, nq),
        in_specs=[
            pl.BlockSpec((tq, xw), lambda b, i: (b * nq + i, 0)),
            pl.BlockSpec((n_mem, xw), lambda b, i: (base + b, 0)),
            pl.BlockSpec((n_mem, xw), lambda b, i: (base + b, 0)),
            pl.BlockSpec((tq, xw), lambda b, i: (b * nq + i, 0)),
        ],
        out_specs=pl.BlockSpec((tq, xw), lambda b, i: (b * nq + i, 0)),
        out_shape=jax.ShapeDtypeStruct((n, xw), qx.dtype),
        compiler_params=_params(2),
        name="xattn",
    )(qx, mem_k, mem_v, sgx)


def _out_proj_kernel(*refs, decode, final, n_heads):
    if decode:
        ol_ref, sga_ref, wuv_ref = refs[:3]
        refs = refs[3:]
    else:
        ya_ref = refs[0]
        refs = refs[1:]
    yb_ref, yc_ref, x_ref, woa_ref, wob_ref, woc_ref = refs[:6]
    refs = refs[6:]
    if final:
        fg_ref, xo_ref, yo_ref = refs
    else:
        (xo_ref,) = refs

    acc = x_ref[...] + _dot(yb_ref[...].astype(BF16), wob_ref[...]) + _dot(yc_ref[...].astype(BF16), woc_ref[...])
    if decode:
        rank, dv = wuv_ref.shape[1:]
        for h in range(n_heads):
            o_a = _dot(ol_ref[:, h * rank:(h + 1) * rank].astype(BF16), wuv_ref[h])
            y_a = (o_a * sga_ref[:, h * dv:(h + 1) * dv]).astype(BF16)
            acc = acc + _dot(y_a, woa_ref[h * dv:(h + 1) * dv, :])
    else:
        acc = acc + _dot(ya_ref[...], woa_ref[...])
    xo_ref[...] = acc
    if final:
        yo_ref[...] = _rms(acc, fg_ref[...])


def _out_proj(first, y_b, y_c, x, lw, final_g, *, decode, tm):
    n, d = x.shape
    weights = [lw["w_oa"], lw["w_ob"], lw["w_oc"]]
    final = final_g is not None
    acts = list(first) + [y_b, y_c, x]
    if decode:
        in_specs = [pl.BlockSpec((tm, first[0].shape[1]), lambda i: (i, 0)),
                    pl.BlockSpec((tm, first[1].shape[1]), lambda i: (i, 0)), _full(first[2].shape)]
    else:
        in_specs = [pl.BlockSpec((tm, first[0].shape[1]), lambda i: (i, 0))]
    in_specs += [pl.BlockSpec((tm, a.shape[1]), lambda i: (i, 0)) for a in (y_b, y_c, x)]
    in_specs += [_full(w.shape) for w in weights]
    args = acts + weights
    n_out = 1
    if final:
        in_specs.append(_full(final_g.shape))
        args.append(final_g)
        n_out = 2
    out = pl.pallas_call(
        functools.partial(_out_proj_kernel, decode=decode, final=final, n_heads=lw["n_heads"]),
        grid=(n // tm,),
        in_specs=in_specs,
        out_specs=[pl.BlockSpec((tm, d), lambda i: (i, 0))] * n_out,
        out_shape=[jax.ShapeDtypeStruct((n, d), F32)] * n_out,
        compiler_params=_params(1),
        name="out_proj_dec" if decode else "out_proj",
    )(*args)
    return out if final else (out[0], None)


def _rope_tables(pos, rope, rows):
    half = rope // 2
    inv = 1.0 / (ROPE_BASE ** (jnp.arange(half, dtype=F32) / half))
    ang = pos.astype(F32)[:, None] * inv[None, :]
    cos, sin = jnp.cos(ang), jnp.sin(ang)
    zeros = jnp.zeros((pos.shape[0], LANES - rope), F32)
    cos_t = jnp.concatenate([cos, cos, zeros], axis=1)
    sin_t = jnp.concatenate([-sin, sin, zeros], axis=1)
    reps = max(rows // pos.shape[0], 1)
    return jnp.tile(cos_t, (reps, 1)), jnp.tile(sin_t, (reps, 1))


def _block_diag(w):
    nb, bs, _ = w.shape
    eye = jnp.eye(nb, dtype=w.dtype)
    return (w[:, :, None, :] * eye[:, None, :, None]).reshape(nb * bs, nb * bs)


def _layer_weights(l, ln_g, w_in, q_norm_g, kv_norm_g, w_q_b, w_kv_b, conv_w, conv_b, w_r, b_r, w_i, b_i,
                   rg_lambda, w_out, rope, x_heads):
    d = w_in.shape[1]
    q_rank, n_heads, qk = w_q_b.shape[1:]
    kv_rank = w_kv_b.shape[1]
    nope = qk - rope
    dv = w_kv_b.shape[3] - nope
    rgw = conv_w.shape[2]
    mla_w = n_heads * dv
    xw = (w_in.shape[2] - q_rank - kv_rank - rope - mla_w - 2 * rgw) // 2
    splits, off = [], 0
    for width in (q_rank, kv_rank, rope, mla_w, rgw, rgw, xw, xw):
        splits.append(w_in[l][:, off:off + width].astype(BF16))
        off += width
    w_qlat, w_c, w_kpe, w_ga, w_xrg, w_grg, w_qx, w_gx = splits
    w_qb = jnp.pad(w_q_b[l], ((0, 0), (0, 0), (0, LANES - rope))).reshape(q_rank, n_heads * (nope + LANES))
    wo = w_out[l].astype(BF16)
    return dict(
        n_heads=n_heads, nope=nope, rope=rope, kv_rank=kv_rank, v_head=dv, x_head_dim=xw // x_heads,
        ln_g=ln_g[l][None, :], w_qlat=w_qlat, q_norm_g=q_norm_g[l][None, :], w_qb=w_qb.astype(BF16),
        w_c=w_c, kv_norm_g=kv_norm_g[l][None, :], w_kpe=jnp.pad(w_kpe, ((0, 0), (0, LANES - rope))),
        w_ga=w_ga, w_xrg=w_xrg, w_grg=w_grg, w_qx=w_qx, w_gx=w_gx,
        w_uk=w_kv_b[l][:, :, :nope].reshape(kv_rank, n_heads * nope).astype(BF16),
        w_uv=w_kv_b[l][:, :, nope:].reshape(kv_rank, n_heads * dv).astype(BF16),
        w_ukt=jnp.transpose(w_kv_b[l][:, :, :nope], (1, 2, 0)).astype(BF16),
        w_uvh=jnp.transpose(w_kv_b[l][:, :, nope:], (1, 0, 2)).astype(BF16),
        conv_w=conv_w[l], conv_b=conv_b[l][None, :],
        w_r=_block_diag(w_r[l]).astype(BF16), b_r=b_r[l].reshape(1, rgw),
        w_i=_block_diag(w_i[l]).astype(BF16), b_i=b_i[l].reshape(1, rgw),
        rg_lambda=rg_lambda[l][None, :],
        w_oa=wo[:mla_w], w_ob=wo[mla_w:mla_w + rgw], w_oc=wo[mla_w + rgw:],
    )


def _pick(n, pref):
    t = min(n, pref)
    while n % t:
        t //= 2
    return t


def kernel(x_prompt, x_sample, mem_prompt, cache_ckv, cache_kpe, state_rg_h, state_rg_conv, cache_mem_k, cache_mem_v, page_table, ln_g, w_in, q_norm_g, kv_norm_g, w_q_b, w_kv_b, conv_w, conv_b, w_r, b_r, w_i, b_i, rg_lambda, mem_norm_g, w_mem_kv, w_out, final_norm_g):
    batch, seq, d = x_prompt.shape
    bd, t_new, _ = x_sample.shape
    depth = w_in.shape[0]
    n_mem = mem_prompt.shape[1]
    x_heads, x_hd = cache_mem_k.shape[3:]
    n_pages = page_table.shape[1]
    page = cache_ckv.shape[2]
    rope = cache_kpe.shape[3]
    n_tap = conv_w.shape[1]
    assert t_new >= n_tap - 1 and seq >= SUBLANES and t_new == SUBLANES

    tm_p = _pick(seq, 512)
    tm_d = _pick(bd * t_new, 256)
    tq = _pick(seq, 512)
    tc = _pick(seq, 256)
    ch = _pick(n_pages, 64)
    cache_kpe_t = jnp.swapaxes(cache_kpe, 2, 3)

    cos_p, sin_p = _rope_tables(jnp.arange(seq, dtype=jnp.int32), rope, tm_p)
    cos_d, sin_d = _rope_tables(n_pages * page + jnp.arange(t_new, dtype=jnp.int32), rope, tm_d)

    xp = x_prompt.reshape(batch * seq, d)
    xs = x_sample.reshape(bd * t_new, d)
    mem = mem_prompt.reshape(batch * n_mem, d)
    fg = final_norm_g[None, :]
    cmk = cache_mem_k.reshape(depth * bd * n_mem, x_heads * x_hd)
    cmv = cache_mem_v.reshape(depth * bd * n_mem, x_heads * x_hd)

    outs = [[] for _ in range(10)]
    yp = ys = None
    for l in range(depth):
        lw = _layer_weights(l, ln_g, w_in, q_norm_g, kv_norm_g, w_q_b, w_kv_b, conv_w, conv_b, w_r, b_r, w_i, b_i,
                            rg_lambda, w_out, rope, x_heads)
        n_heads = lw["n_heads"]
        last = l == depth - 1

        mk, mv = _mem_kv(mem, mem_norm_g[l][None, :], w_mem_kv[l].astype(BF16), tm=_pick(batch * n_mem, 512))
        qf, c_p, kpe_p, kn, kpb, v, sga, xrg, sgr, qx, sgx = _in_proj(xp, lw, cos_p, sin_p, decode=False, tm=tm_p,
                                                                      seq=seq)
        y_a = _attn_prompt(qf, kn, kpb, v, sga, batch=batch, seq=seq, n_heads=n_heads, tq=tq,
                           hp=2 if n_heads % 2 == 0 else 1)
        y_b, h_p, cv_p = _rg_prompt(xrg, sgr, lw, batch=batch, seq=seq, tc=tc)
        y_c = _xattn(qx, mk, mv, sgx, batch=batch, seq=seq, n_mem=n_mem, n_heads=x_heads, tq=_pick(seq, 512))
        xp, yp = _out_proj([y_a], y_b, y_c, xp, lw, fg if last else None, decode=False, tm=tm_p)

        qc, qpe, c_s, kpe_s, sga, xrg, sgr, qx, sgx = _in_proj(xs, lw, cos_d, sin_d, decode=True, tm=tm_d, seq=t_new)
        o_lat = _attn_decode(page_table, qc, qpe, c_s, kpe_s, cache_ckv, cache_kpe_t, layer=l, t_new=t_new,
                             n_heads=n_heads, ch=ch)
        rgw = xrg.shape[1]
        to_tm = lambda a: jnp.transpose(a.reshape(bd, -1, rgw), (1, 0, 2))
        y_tm, h_s, cv_tm = _rg_decode(to_tm(xrg), to_tm(sgr), jnp.transpose(state_rg_conv[l], (1, 0, 2)),
                                      state_rg_h[l], lw)
        y_b = jnp.transpose(y_tm, (1, 0, 2)).reshape(bd * t_new, rgw)
        y_c = _xattn(qx, cmk, cmv, sgx, batch=bd, seq=t_new, n_mem=n_mem, n_heads=x_heads, tq=t_new, base=l * bd)
        xs, ys = _out_proj([o_lat, sga, lw["w_uvh"]], y_b, y_c, xs, lw, fg if last else None, decode=True, tm=tm_d)

        for lst, val in zip(outs, (
                c_p.reshape(batch, seq, -1), kpe_p.reshape(batch, seq, -1), h_p[:, 0], cv_p[:, SUBLANES - (n_tap - 1):],
                mk.reshape(batch, n_mem, x_heads, x_hd), mv.reshape(batch, n_mem, x_heads, x_hd),
                c_s.reshape(bd, t_new, -1), kpe_s.reshape(bd, t_new, -1), h_s, jnp.transpose(cv_tm, (1, 0, 2)))):
            lst.append(val)

    return (yp.reshape(batch, seq, d), ys.reshape(bd, t_new, d)) + tuple(jnp.stack(o) for o in outs)
```

```python
import functools

import jax
import jax.numpy as jnp
from jax import lax
from jax.experimental import pallas as pl
from jax.experimental.pallas import tpu as pltpu

EPS = 1e-6
ROPE_BASE = 10000.0
RG_C = 8.0
LANES = 128
SUBLANES = 8
NEG_BIG = -1e30
VMEM_LIMIT = 56 * 1024 * 1024

F32 = jnp.float32
BF16 = jnp.bfloat16

_NT = (((1,), (1,)), ((), ()))


def _params(n_grid, vmem=VMEM_LIMIT):
    return pltpu.CompilerParams(dimension_semantics=("arbitrary",) * n_grid, vmem_limit_bytes=vmem)


def _full(shape):
    n = len(shape)
    return pl.BlockSpec(shape, lambda *_: (0,) * n)


def _rms(x, g):
    return x * lax.rsqrt(jnp.mean(x * x, axis=-1, keepdims=True) + EPS) * g


def _sigmoid(x):
    return 1.0 / (1.0 + jnp.exp(-x))


def _silu(x):
    return x * _sigmoid(x)


def _dot(a, b):
    return jnp.dot(a, b, preferred_element_type=F32)


def _dot_nt(a, b):
    return lax.dot_general(a, b, _NT, preferred_element_type=F32)


def _rope_padded(x, cos_t, sin_t, half):
    swapped = pltpu.roll(x, half, 1) + pltpu.roll(x, LANES - half, 1)
    return x * cos_t + swapped * sin_t


def _in_proj_kernel(*refs, decode, n_heads, nope, half, scale, x_scale):
    (x_ref, lng_ref, cos_ref, sin_ref, wq_ref, qg_ref, wqb_ref, wc_ref, cg_ref, wk_ref,
     wa_ref, wxr_ref, wgr_ref, wqx_ref, wgx_ref) = refs[:15]
    if decode:
        wukt_ref = refs[15]
        (qc_ref, qpe_ref, c_ref, kpe_ref, sga_ref, xrg_ref, sgr_ref, qx_ref, sgx_ref) = refs[16:]
    else:
        wuk_ref, wuv_ref = refs[15:17]
        (qf_ref, c_ref, kpe_ref, kn_ref, kpb_ref, v_ref, sga_ref, xrg_ref, sgr_ref, qx_ref,
         sgx_ref) = refs[17:]

    u = _rms(x_ref[...], lng_ref[...]).astype(BF16)
    cos_t = cos_ref[...]
    sin_t = sin_ref[...]

    qn = _rms(_dot(u, wq_ref[...]), qg_ref[...]).astype(BF16)
    hw = nope + LANES
    for h in range(n_heads):
        qh = _dot(qn, wqb_ref[:, h * hw:(h + 1) * hw])
        q_nope = qh[:, :nope] * scale
        q_pe = _rope_padded(qh[:, nope:], cos_t, sin_t, half) * scale
        if decode:
            qc_ref[:, h * wukt_ref.shape[2]:(h + 1) * wukt_ref.shape[2]] = _dot(q_nope.astype(BF16), wukt_ref[h])
            qpe_ref[:, h * LANES:(h + 1) * LANES] = q_pe
        else:
            qf_ref[:, h * hw:h * hw + nope] = q_nope.astype(BF16)
            qf_ref[:, h * hw + nope:(h + 1) * hw] = q_pe.astype(BF16)

    c_new = _rms(_dot(u, wc_ref[...]), cg_ref[...])
    c_ref[...] = c_new
    kpe = _rope_padded(_dot(u, wk_ref[...]), cos_t, sin_t, half)
    kpe_ref[...] = kpe[:, :kpe_ref.shape[1]]
    if not decode:
        cb = c_new.astype(BF16)
        kn_ref[...] = _dot(cb, wuk_ref[...]).astype(BF16)
        v_ref[...] = _dot(cb, wuv_ref[...]).astype(BF16)
        kpb_ref[...] = kpe.astype(BF16)

    sga_ref[...] = _silu(_dot(u, wa_ref[...])).astype(sga_ref.dtype)
    xrg_ref[...] = _dot(u, wxr_ref[...])
    sgr_ref[...] = _silu(_dot(u, wgr_ref[...])).astype(sgr_ref.dtype)
    qx_ref[...] = (_dot(u, wqx_ref[...]) * x_scale).astype(qx_ref.dtype)
    sgx_ref[...] = _silu(_dot(u, wgx_ref[...])).astype(sgx_ref.dtype)


def _in_proj(x, lw, cos_t, sin_t, *, decode, tm, seq):
    n, d = x.shape
    n_heads, nope, rank, rope = lw["n_heads"], lw["nope"], lw["kv_rank"], lw["rope"]
    dv = lw["v_head"]
    rgw, xw = lw["w_xrg"].shape[1], lw["w_qx"].shape[1]
    act = F32 if decode else BF16
    tab_rows = cos_t.shape[0]
    n_tab = tab_rows // tm

    def row(width):
        return pl.BlockSpec((tm, width), lambda i: (i, 0))

    tab = pl.BlockSpec((tm, LANES), lambda i: (i % n_tab, 0))
    weights = [lw["w_qlat"], lw["q_norm_g"], lw["w_qb"], lw["w_c"], lw["kv_norm_g"], lw["w_kpe"],
               lw["w_ga"], lw["w_xrg"], lw["w_grg"], lw["w_qx"], lw["w_gx"]]
    if decode:
        weights += [lw["w_ukt"]]
        out_w = [(n_heads * rank, F32), (n_heads * LANES, F32), (rank, F32), (rope, F32), (n_heads * dv, act),
                 (rgw, F32), (rgw, act), (xw, act), (xw, act)]
    else:
        weights += [lw["w_uk"], lw["w_uv"]]
        out_w = [(n_heads * (nope + LANES), BF16), (rank, F32), (rope, F32), (n_heads * nope, BF16),
                 (LANES, BF16), (n_heads * dv, BF16), (n_heads * dv, act), (rgw, F32), (rgw, act), (xw, act),
                 (xw, act)]
    kern = functools.partial(_in_proj_kernel, decode=decode, n_heads=n_heads, nope=nope, half=rope // 2,
                             scale=float((nope + rope) ** -0.5), x_scale=float(lw["x_head_dim"] ** -0.5))
    return pl.pallas_call(
        kern,
        grid=(n // tm,),
        in_specs=[row(d), _full(lw["ln_g"].shape), tab, tab] + [_full(w.shape) for w in weights],
        out_specs=[row(w) for w, _ in out_w],
        out_shape=[jax.ShapeDtypeStruct((n, w), dt) for w, dt in out_w],
        compiler_params=_params(1),
        name="in_proj_dec" if decode else "in_proj",
    )(x, lw["ln_g"], cos_t, sin_t, *weights)


def _mem_kv_kernel(m_ref, g_ref, w_ref, k_ref, v_ref):
    u = _rms(m_ref[...], g_ref[...]).astype(BF16)
    kv = _dot(u, w_ref[...])
    xw = k_ref.shape[1]
    k_ref[...] = kv[:, :xw]
    v_ref[...] = kv[:, xw:]


def _mem_kv(mem, g, w, *, tm):
    n, d = mem.shape
    xw = w.shape[1] // 2
    return pl.pallas_call(
        _mem_kv_kernel,
        grid=(n // tm,),
        in_specs=[pl.BlockSpec((tm, d), lambda i: (i, 0)), _full(g.shape), _full(w.shape)],
        out_specs=[pl.BlockSpec((tm, xw), lambda i: (i, 0))] * 2,
        out_shape=[jax.ShapeDtypeStruct((n, xw), F32)] * 2,
        compiler_params=_params(1),
        name="mem_kv",
    )(mem, g, w)


def _attn_prompt_kernel(q_ref, kn_ref, kp_ref, v_ref, g_ref, o_ref, *, tq, hp):
    qi = pl.program_id(2)
    qw = q_ref.shape[1] // hp
    kw = kn_ref.shape[1] // hp
    dv = v_ref.shape[1] // hp

    def scores(j, h):
        off = pl.multiple_of(j * tq, tq)
        k = jnp.concatenate([kn_ref[pl.ds(off, tq), h * kw:(h + 1) * kw], kp_ref[pl.ds(off, tq), :]], axis=1)
        return _dot_nt(q_ref[:, h * qw:(h + 1) * qw], k), v_ref[pl.ds(off, tq), h * dv:(h + 1) * dv]

    def update(s, v, carry):
        m, l, acc = carry
        m_new = jnp.maximum(m, jnp.max(s, axis=-1, keepdims=True))
        alpha = jnp.exp(m - m_new)
        p = jnp.exp(s - m_new)
        l = alpha * l + jnp.sum(p, axis=-1, keepdims=True)
        acc = alpha * acc + _dot(p.astype(BF16), v)
        return m_new, l, acc

    def body(j, carry):
        return tuple(update(*scores(j, h), carry[h]) for h in range(hp))

    init = tuple((jnp.full((tq, 1), NEG_BIG, F32), jnp.zeros((tq, 1), F32), jnp.zeros((tq, dv), F32))
                 for _ in range(hp))
    carry = lax.fori_loop(0, qi, body, init)
    row = lax.broadcasted_iota(jnp.int32, (tq, tq), 0)
    col = lax.broadcasted_iota(jnp.int32, (tq, tq), 1)
    for h in range(hp):
        s, v = scores(qi, h)
        _, l, acc = update(jnp.where(col <= row, s, NEG_BIG), v, carry[h])
        sl = slice(h * dv, (h + 1) * dv)
        o_ref[:, sl] = (acc / l * g_ref[:, sl].astype(F32)).astype(o_ref.dtype)


def _attn_prompt(qf, kn, kpb, v, sga, *, batch, seq, n_heads, tq, hp):
    n = qf.shape[0]
    qw = qf.shape[1] // n_heads * hp
    kw = kn.shape[1] // n_heads * hp
    dv = v.shape[1] // n_heads * hp
    nq = seq // tq
    return pl.pallas_call(
        functools.partial(_attn_prompt_kernel, tq=tq, hp=hp),
        grid=(batch, n_heads // hp, nq),
        in_specs=[
            pl.BlockSpec((tq, qw), lambda b, h, i: (b * nq + i, h)),
            pl.BlockSpec((seq, kw), lambda b, h, i: (b, h)),
            pl.BlockSpec((seq, LANES), lambda b, h, i: (b, 0)),
            pl.BlockSpec((seq, dv), lambda b, h, i: (b, h)),
            pl.BlockSpec((tq, dv), lambda b, h, i: (b * nq + i, h)),
        ],
        out_specs=pl.BlockSpec((tq, dv), lambda b, h, i: (b * nq + i, h)),
        out_shape=jax.ShapeDtypeStruct((n, v.shape[1]), BF16),
        compiler_params=_params(3),
        name="attn_prompt",
    )(qf, kn, kpb, v, sga)


def _attn_decode_kernel(pt_ref, qc_ref, qpe_ref, cn_ref, kn_ref, ckv_hbm, kpe_hbm, o_ref,
                        kbuf, pbuf, sem, q_s, qp_s, m_s, l_s, acc_s, *, layer, ch, n_heads, n_split):
    b = pl.program_id(0)
    c = pl.program_id(1)
    nb = pl.num_programs(0)
    nc = pl.num_programs(1)
    g = b * nc + c
    slot = g % 2
    page = ckv_hbm.shape[2]
    t_new, rank = cn_ref.shape
    rope = kn_ref.shape[1]

    def copies(bb, cc, sl):
        out = []
        for p in range(ch):
            pg = pt_ref[bb, cc * ch + p]
            out.append(pltpu.make_async_copy(ckv_hbm.at[layer, pg], kbuf.at[sl, pl.ds(p * page, page), :],
                                             sem.at[sl, 0]))
            out.append(pltpu.make_async_copy(kpe_hbm.at[layer, pg], pbuf.at[sl, :, pl.ds(p * page, page)],
                                             sem.at[sl, 1]))
        return out

    @pl.when(g == 0)
    def _():
        for cp in copies(b, c, slot):
            cp.start()

    @pl.when(g + 1 < nb * nc)
    def _():
        wrap = c + 1 == nc
        for cp in copies(jnp.where(wrap, b + 1, b), jnp.where(wrap, 0, c + 1), 1 - slot):
            cp.start()

    @pl.when(c == 0)
    def _():
        qc = qc_ref[...]
        qp = qpe_ref[...]
        q_s[...] = jnp.concatenate([qc[:, h * rank:(h + 1) * rank] for h in range(n_heads)], axis=0).astype(BF16)
        qp_s[...] = jnp.concatenate([qp[:, h * LANES:(h + 1) * LANES] for h in range(n_heads)], axis=0).astype(BF16)
        m_s[...] = jnp.full(m_s.shape, NEG_BIG, F32)
        l_s[...] = jnp.zeros(l_s.shape, F32)
        acc_s[...] = jnp.zeros(acc_s.shape, F32)

    for cp in copies(b, c, slot):
        cp.wait()

    q = q_s[...]
    qp = qp_s[...][:, :rope]

    def piece(s, kb):
        m = jnp.max(s, axis=-1, keepdims=True)
        p = jnp.exp(s - m)
        return m, jnp.sum(p, axis=-1, keepdims=True), _dot(p.astype(BF16), kb)

    def update(parts):
        m_new = m_s[...]
        for m, _, _ in parts:
            m_new = jnp.maximum(m_new, m)
        alpha = jnp.exp(m_s[...] - m_new)
        l, acc = alpha * l_s[...], alpha * acc_s[...]
        for m, l_p, acc_p in parts:
            w = jnp.exp(m - m_new)
            l, acc = l + w * l_p, acc + w * acc_p
        m_s[...], l_s[...], acc_s[...] = m_new, l, acc

    keys = ch * page
    parts = []
    for lo in range(0, keys, keys // n_split):
        kb = kbuf[slot, lo:lo + keys // n_split, :].astype(BF16)
        pb = pbuf[slot, :, lo:lo + keys // n_split].astype(BF16)
        parts.append(piece(_dot_nt(q, kb) + _dot(qp, pb), kb))
    update(parts)

    @pl.when(c == nc - 1)
    def _():
        pad = LANES - t_new
        kb2 = jnp.concatenate([cn_ref[...], jnp.zeros((pad, rank), F32)], axis=0).astype(BF16)
        pb2 = jnp.concatenate([kn_ref[...], jnp.zeros((pad, rope), F32)], axis=0).astype(BF16)
        s = _dot_nt(q, kb2) + _dot_nt(qp, pb2)
        rows = s.shape[0]
        row = lax.broadcasted_iota(jnp.int32, (rows, LANES), 0)
        col = lax.broadcasted_iota(jnp.int32, (rows, LANES), 1)
        s = jnp.where(col <= row % t_new, s, NEG_BIG)
        update([piece(s, kb2)])
        o = acc_s[...] / l_s[...]
        for h in range(n_heads):
            o_ref[:, h * rank:(h + 1) * rank] = o[h * t_new:(h + 1) * t_new, :]


def _attn_decode(page_table, qc, qpe, c_new, kpe_new, cache_ckv, cache_kpe_t, *, layer, t_new, n_heads, ch):
    n = qc.shape[0]
    batch = n // t_new
    n_pages = page_table.shape[1]
    page, rank = cache_ckv.shape[2:]
    rope = cache_kpe_t.shape[2]
    nc = n_pages // ch
    rows = n_heads * t_new

    def per_batch(width):
        return pl.BlockSpec((t_new, width), lambda b, c, pt: (b, 0))

    grid_spec = pltpu.PrefetchScalarGridSpec(
        num_scalar_prefetch=1,
        grid=(batch, nc),
        in_specs=[per_batch(qc.shape[1]), per_batch(qpe.shape[1]), per_batch(rank), per_batch(rope),
                  pl.BlockSpec(memory_space=pl.ANY), pl.BlockSpec(memory_space=pl.ANY)],
        out_specs=per_batch(n_heads * rank),
        scratch_shapes=[
            pltpu.VMEM((2, ch * page, rank), F32),
            pltpu.VMEM((2, rope, ch * page), F32),
            pltpu.SemaphoreType.DMA((2, 2)),
            pltpu.VMEM((rows, rank), BF16),
            pltpu.VMEM((rows, LANES), BF16),
            pltpu.VMEM((rows, 1), F32),
            pltpu.VMEM((rows, 1), F32),
            pltpu.VMEM((rows, rank), F32),
        ],
    )
    return pl.pallas_call(
        functools.partial(_attn_decode_kernel, layer=layer, ch=ch, n_heads=n_heads, n_split=2 if ch % 2 == 0 else 1),
        grid_spec=grid_spec,
        out_shape=jax.ShapeDtypeStruct((n, n_heads * rank), F32),
        compiler_params=_params(2),
        name="attn_decode",
    )(page_table, qc, qpe, c_new, kpe_new, cache_ckv, cache_kpe_t)


def _rg_gates(xc, wr, br, wi, bi, lam):
    xb = xc.astype(BF16)
    r = _sigmoid(_dot(xb, wr) + br)
    ig = _sigmoid(_dot(xb, wi) + bi)
    z = -lam
    softplus = jnp.maximum(z, 0.0) + jnp.log1p(jnp.exp(-jnp.abs(z)))
    log_a = -RG_C * r * softplus
    a = jnp.exp(log_a)
    bx = jnp.sqrt(jnp.tanh(-log_a) * (1.0 + a * a)) * (ig * xc)
    return a, bx


def _rg_prompt_kernel(x_ref, halo_ref, g_ref, cw_ref, cb_ref, wr_ref, br_ref, wi_ref, bi_ref, lam_ref,
                      y_ref, hl_ref, cn_ref, xs, a_s, b_s, h_s, hc_s, *, tc):
    i = pl.program_id(1)
    width = x_ref.shape[1]
    n_tap = cw_ref.shape[0]
    x = x_ref[...]
    xs[SUBLANES:SUBLANES + tc, :] = x

    @pl.when(i == 0)
    def _():
        xs[0:SUBLANES, :] = jnp.zeros((SUBLANES, width), F32)
        hc_s[...] = jnp.zeros(hc_s.shape, F32)

    @pl.when(i > 0)
    def _():
        xs[0:SUBLANES, :] = halo_ref[...]

    xc = cb_ref[...]
    for k in range(n_tap):
        start = SUBLANES - (n_tap - 1) + k
        xc = xc + xs[start:start + tc, :] * cw_ref[k:k + 1, :]
    a, bx = _rg_gates(xc, wr_ref[...], br_ref[...], wi_ref[...], bi_ref[...], lam_ref[...])
    a_s[...] = a
    b_s[...] = bx

    row = lax.broadcasted_iota(jnp.int32, (SUBLANES, width), 0)

    def group(gi, h_prev):
        off = pl.multiple_of(gi * SUBLANES, SUBLANES)
        ag = a_s[pl.ds(off, SUBLANES), :]
        bg = b_s[pl.ds(off, SUBLANES), :]
        d = 1
        while d < SUBLANES:
            keep = row >= d
            bg = bg + ag * jnp.where(keep, pltpu.roll(bg, d, 0), 0.0)
            ag = ag * jnp.where(keep, pltpu.roll(ag, d, 0), 1.0)
            d *= 2
        h = bg + ag * h_prev
        h_s[pl.ds(off, SUBLANES), :] = h
        return jnp.broadcast_to(h[SUBLANES - 1:SUBLANES, :], (SUBLANES, width))

    h_last = lax.fori_loop(0, tc // SUBLANES, group, hc_s[...], unroll=2)
    hc_s[...] = h_last
    y_ref[...] = (h_s[...] * g_ref[...].astype(F32)).astype(y_ref.dtype)
    hl_ref[0] = h_last[0:1, :]
    cn_ref[0] = x[tc - SUBLANES:, :]


def _rg_prompt(x_rg, sg_rg, lw, *, batch, seq, tc):
    n, width = x_rg.shape
    nc = seq // tc
    per8 = tc // SUBLANES
    weights = [lw["conv_w"], lw["conv_b"], lw["w_r"], lw["b_r"], lw["w_i"], lw["b_i"], lw["rg_lambda"]]
    return pl.pallas_call(
        functools.partial(_rg_prompt_kernel, tc=tc),
        grid=(batch, nc),
        in_specs=[
            pl.BlockSpec((tc, width), lambda b, i: (b * nc + i, 0)),
            pl.BlockSpec((SUBLANES, width), lambda b, i: (jnp.maximum((b * nc + i) * per8 - 1, 0), 0)),
            pl.BlockSpec((tc, width), lambda b, i: (b * nc + i, 0)),
        ] + [_full(w.shape) for w in weights],
        out_specs=[
            pl.BlockSpec((tc, width), lambda b, i: (b * nc + i, 0)),
            pl.BlockSpec((1, 1, width), lambda b, i: (b, 0, 0)),
            pl.BlockSpec((1, SUBLANES, width), lambda b, i: (b, 0, 0)),
        ],
        out_shape=[
            jax.ShapeDtypeStruct((n, width), BF16),
            jax.ShapeDtypeStruct((batch, 1, width), F32),
            jax.ShapeDtypeStruct((batch, SUBLANES, width), F32),
        ],
        scratch_shapes=[
            pltpu.VMEM((tc + SUBLANES, width), F32),
            pltpu.VMEM((tc, width), F32),
            pltpu.VMEM((tc, width), F32),
            pltpu.VMEM((tc, width), F32),
            pltpu.VMEM((SUBLANES, width), F32),
        ],
        compiler_params=_params(2),
        name="rg_prompt",
    )(x_rg, x_rg, sg_rg, *weights)


def _rg_decode_kernel(x_ref, g_ref, c0_ref, h0_ref, cw_ref, cb_ref, wr_ref, br_ref, wi_ref, bi_ref, lam_ref,
                      y_ref, hl_ref, cn_ref):
    t_new = x_ref.shape[0]
    n_tap = cw_ref.shape[0]
    xp = [c0_ref[j] for j in range(n_tap - 1)] + [x_ref[t] for t in range(t_new)]
    h = h0_ref[...]
    for t in range(t_new):
        xc = cb_ref[...]
        for k in range(n_tap):
            xc = xc + xp[t + k] * cw_ref[k:k + 1, :]
        a, bx = _rg_gates(xc, wr_ref[...], br_ref[...], wi_ref[...], bi_ref[...], lam_ref[...])
        h = a * h + bx
        y_ref[t] = h * g_ref[t]
    hl_ref[...] = h
    for j in range(n_tap - 1):
        cn_ref[j] = xp[t_new + j]


def _rg_decode(x_tm, g_tm, conv0_tm, h0, lw):
    weights = [lw["conv_w"], lw["conv_b"], lw["w_r"], lw["b_r"], lw["w_i"], lw["b_i"], lw["rg_lambda"]]
    args = [x_tm, g_tm, conv0_tm, h0] + weights
    return pl.pallas_call(
        _rg_decode_kernel,
        in_specs=[_full(a.shape) for a in args],
        out_specs=[_full(x_tm.shape), _full(h0.shape), _full(conv0_tm.shape)],
        out_shape=[jax.ShapeDtypeStruct(x_tm.shape, F32), jax.ShapeDtypeStruct(h0.shape, F32),
                   jax.ShapeDtypeStruct(conv0_tm.shape, F32)],
        grid=(1,),
        compiler_params=_params(1),
        name="rg_decode",
    )(*args)


def _xattn_kernel(q_ref, k_ref, v_ref, g_ref, o_ref, *, n_heads):
    tq = q_ref.shape[0]
    hd = q_ref.shape[1] // n_heads
    min_rows = 2 * SUBLANES
    for h in range(n_heads):
        sl = slice(h * hd, (h + 1) * hd)
        q = q_ref[:, sl]
        if tq < min_rows:
            q = jnp.concatenate([q.astype(F32), jnp.zeros((min_rows - tq, hd), F32)], axis=0)
        k = k_ref[:, sl].astype(BF16)
        v = v_ref[:, sl].astype(BF16)
        s = _dot_nt(q.astype(BF16), k)
        p = jnp.exp(s - jnp.max(s, axis=-1, keepdims=True))
        o = _dot(p.astype(BF16), v) / jnp.sum(p, axis=-1, keepdims=True)
        o_ref[:, sl] = (o[:tq] * g_ref[:, sl].astype(F32)).astype(o_ref.dtype)


def _xattn_decode_kernel(q_ref, k_ref, v_ref, g_ref, o_ref, *, t_new, n_heads, bb):
    hd = k_ref.shape[1]
    kv_rows = k_ref.shape[0] // bb
    grp = 2 * SUBLANES
    pad = jnp.zeros((grp - t_new, hd), F32)
    row = lax.broadcasted_iota(jnp.int32, (n_heads * grp, kv_rows), 0)
    col = lax.broadcasted_iota(jnp.int32, (n_heads * grp, kv_rows), 1)
    own = col % n_heads == row // grp
    for j in range(bb):
        rows = slice(j * t_new, (j + 1) * t_new)
        k = k_ref[j * kv_rows:(j + 1) * kv_rows, :].astype(BF16)
        v = v_ref[j * kv_rows:(j + 1) * kv_rows, :].astype(BF16)
        pieces = []
        for h in range(n_heads):
            pieces += [q_ref[rows, h * hd:(h + 1) * hd], pad]
        q = jnp.concatenate(pieces, axis=0).astype(BF16)
        s = jnp.where(own, _dot_nt(q, k), NEG_BIG)
        p = jnp.exp(s - jnp.max(s, axis=-1, keepdims=True))
        o = _dot(p.astype(BF16), v) / jnp.sum(p, axis=-1, keepdims=True)
        for h in range(n_heads):
            sl = slice(h * hd, (h + 1) * hd)
            o_ref[rows, sl] = o[h * grp:h * grp + t_new] * g_ref[rows, sl]


def _xattn_decode(qx, cache_k, cache_v, sgx, *, layer, t_new, bb):
    n, xw = qx.shape
    _, bd, n_mem, n_heads, hd = cache_k.shape
    kv_rows = n_mem * n_heads
    rows = pl.BlockSpec((bb * t_new, xw), lambda i: (i, 0))
    mem = pl.BlockSpec((bb * kv_rows, hd), lambda i: (layer * (bd // bb) + i, 0))
    return pl.pallas_call(
        functools.partial(_xattn_decode_kernel, t_new=t_new, n_heads=n_heads, bb=bb),
        grid=(bd // bb,),
        in_specs=[rows, mem, mem, rows],
        out_specs=rows,
        out_shape=jax.ShapeDtypeStruct((n, xw), F32),
        compiler_params=_params(1),
        name="xattn_dec",
    )(qx, cache_k.reshape(-1, hd), cache_v.reshape(-1, hd), sgx)


def _xattn(qx, mem_k, mem_v, sgx, *, batch, seq, n_mem, n_heads, tq, base=0):
    n, xw = qx.shape
    nq = seq // tq
    return pl.pallas_call(
        functools.partial(_xattn_kernel, n_heads=n_heads),
        grid=(batch, nq),
        in_specs=[
            pl.BlockSpec((tq, xw), lambda b, i: (b * nq + i, 0)),
            pl.BlockSpec((n_mem, xw), lambda b, i: (base + b, 0)),
            pl.BlockSpec((n_mem, xw), lambda b, i: (base + b, 0)),
            pl.BlockSpec((tq, xw), lambda b, i: (b * nq + i, 0)),
        ],
        out_specs=pl.BlockSpec((tq, xw), lambda b, i: (b * nq + i, 0)),
        out_shape=jax.ShapeDtypeStruct((n, xw), qx.dtype),
        compiler_params=_params(2),
        name="xattn",
    )(qx, mem_k, mem_v, sgx)


def _out_proj_kernel(*refs, decode, final, n_heads):
    if decode:
        ol_ref, sga_ref, wuv_ref = refs[:3]
        refs = refs[3:]
    else:
        ya_ref = refs[0]
        refs = refs[1:]
    yb_ref, yc_ref, x_ref, woa_ref, wob_ref, woc_ref = refs[:6]
    refs = refs[6:]
    if final:
        fg_ref, xo_ref, yo_ref = refs
    else:
        (xo_ref,) = refs

    acc = x_ref[...] + _dot(yb_ref[...].astype(BF16), wob_ref[...]) + _dot(yc_ref[...].astype(BF16), woc_ref[...])
    if decode:
        rank, dv = wuv_ref.shape[1:]
        for h in range(n_heads):
            o_a = _dot(ol_ref[:, h * rank:(h + 1) * rank].astype(BF16), wuv_ref[h])
            y_a = (o_a * sga_ref[:, h * dv:(h + 1) * dv]).astype(BF16)
            acc = acc + _dot(y_a, woa_ref[h * dv:(h + 1) * dv, :])
    else:
        acc = acc + _dot(ya_ref[...], woa_ref[...])
    xo_ref[...] = acc
    if final:
        yo_ref[...] = _rms(acc, fg_ref[...])


def _out_proj(first, y_b, y_c, x, lw, final_g, *, decode, tm):
    n, d = x.shape
    weights = [lw["w_oa"], lw["w_ob"], lw["w_oc"]]
    final = final_g is not None
    acts = list(first) + [y_b, y_c, x]
    if decode:
        in_specs = [pl.BlockSpec((tm, first[0].shape[1]), lambda i: (i, 0)),
                    pl.BlockSpec((tm, first[1].shape[1]), lambda i: (i, 0)), _full(first[2].shape)]
    else:
        in_specs = [pl.BlockSpec((tm, first[0].shape[1]), lambda i: (i, 0))]
    in_specs += [pl.BlockSpec((tm, a.shape[1]), lambda i: (i, 0)) for a in (y_b, y_c, x)]
    in_specs += [_full(w.shape) for w in weights]
    args = acts + weights
    n_out = 1
    if final:
        in_specs.append(_full(final_g.shape))
        args.append(final_g)
        n_out = 2
    out = pl.pallas_call(
        functools.partial(_out_proj_kernel, decode=decode, final=final, n_heads=lw["n_heads"]),
        grid=(n // tm,),
        in_specs=in_specs,
        out_specs=[pl.BlockSpec((tm, d), lambda i: (i, 0))] * n_out,
        out_shape=[jax.ShapeDtypeStruct((n, d), F32)] * n_out,
        compiler_params=_params(1),
        name="out_proj_dec" if decode else "out_proj",
    )(*args)
    return out if final else (out[0], None)


def _rope_tables(pos, rope, rows):
    half = rope // 2
    inv = 1.0 / (ROPE_BASE ** (jnp.arange(half, dtype=F32) / half))
    ang = pos.astype(F32)[:, None] * inv[None, :]
    cos, sin = jnp.cos(ang), jnp.sin(ang)
    zeros = jnp.zeros((pos.shape[0], LANES - rope), F32)
    cos_t = jnp.concatenate([cos, cos, zeros], axis=1)
    sin_t = jnp.concatenate([-sin, sin, zeros], axis=1)
    reps = max(rows // pos.shape[0], 1)
    return jnp.tile(cos_t, (reps, 1)), jnp.tile(sin_t, (reps, 1))


def _block_diag(w):
    nb, bs, _ = w.shape
    eye = jnp.eye(nb, dtype=w.dtype)
    return (w[:, :, None, :] * eye[:, None, :, None]).reshape(nb * bs, nb * bs)


def _layer_weights(l, ln_g, w_in, q_norm_g, kv_norm_g, w_q_b, w_kv_b, conv_w, conv_b, w_r, b_r, w_i, b_i,
                   rg_lambda, w_out, rope, x_heads):
    d = w_in.shape[1]
    q_rank, n_heads, qk = w_q_b.shape[1:]
    kv_rank = w_kv_b.shape[1]
    nope = qk - rope
    dv = w_kv_b.shape[3] - nope
    rgw = conv_w.shape[2]
    mla_w = n_heads * dv
    xw = (w_in.shape[2] - q_rank - kv_rank - rope - mla_w - 2 * rgw) // 2
    splits, off = [], 0
    for width in (q_rank, kv_rank, rope, mla_w, rgw, rgw, xw, xw):
        splits.append(w_in[l][:, off:off + width].astype(BF16))
        off += width
    w_qlat, w_c, w_kpe, w_ga, w_xrg, w_grg, w_qx, w_gx = splits
    w_qb = jnp.pad(w_q_b[l], ((0, 0), (0, 0), (0, LANES - rope))).reshape(q_rank, n_heads * (nope + LANES))
    wo = w_out[l].astype(BF16)
    return dict(
        n_heads=n_heads, nope=nope, rope=rope, kv_rank=kv_rank, v_head=dv, x_head_dim=xw // x_heads,
        ln_g=ln_g[l][None, :], w_qlat=w_qlat, q_norm_g=q_norm_g[l][None, :], w_qb=w_qb.astype(BF16),
        w_c=w_c, kv_norm_g=kv_norm_g[l][None, :], w_kpe=jnp.pad(w_kpe, ((0, 0), (0, LANES - rope))),
        w_ga=w_ga, w_xrg=w_xrg, w_grg=w_grg, w_qx=w_qx, w_gx=w_gx,
        w_uk=w_kv_b[l][:, :, :nope].reshape(kv_rank, n_heads * nope).astype(BF16),
        w_uv=w_kv_b[l][:, :, nope:].reshape(kv_rank, n_heads * dv).astype(BF16),
        w_ukt=jnp.transpose(w_kv_b[l][:, :, :nope], (1, 2, 0)).astype(BF16),
        w_uvh=jnp.transpose(w_kv_b[l][:, :, nope:], (1, 0, 2)).astype(BF16),
        conv_w=conv_w[l], conv_b=conv_b[l][None, :],
        w_r=_block_diag(w_r[l]).astype(BF16), b_r=b_r[l].reshape(1, rgw),
        w_i=_block_diag(w_i[l]).astype(BF16), b_i=b_i[l].reshape(1, rgw),
        rg_lambda=rg_lambda[l][None, :],
        w_oa=wo[:mla_w], w_ob=wo[mla_w:mla_w + rgw], w_oc=wo[mla_w + rgw:],
    )


def _pick(n, pref):
    t = min(n, pref)
    while n % t:
        t //= 2
    return t


def kernel(x_prompt, x_sample, mem_prompt, cache_ckv, cache_kpe, state_rg_h, state_rg_conv, cache_mem_k, cache_mem_v, page_table, ln_g, w_in, q_norm_g, kv_norm_g, w_q_b, w_kv_b, conv_w, conv_b, w_r, b_r, w_i, b_i, rg_lambda, mem_norm_g, w_mem_kv, w_out, final_norm_g):
    batch, seq, d = x_prompt.shape
    bd, t_new, _ = x_sample.shape
    depth = w_in.shape[0]
    n_mem = mem_prompt.shape[1]
    x_heads, x_hd = cache_mem_k.shape[3:]
    n_pages = page_table.shape[1]
    page = cache_ckv.shape[2]
    rope = cache_kpe.shape[3]
    n_tap = conv_w.shape[1]
    assert t_new >= n_tap - 1 and seq >= SUBLANES and t_new == SUBLANES

    tm_p = _pick(seq, 512)
    tm_d = _pick(bd * t_new, 256)
    tq = _pick(seq, 512)
    tc = _pick(seq, 256)
    ch = _pick(n_pages, 64)
    cache_kpe_t = jnp.swapaxes(cache_kpe, 2, 3)

    cos_p, sin_p = _rope_tables(jnp.arange(seq, dtype=jnp.int32), rope, tm_p)
    cos_d, sin_d = _rope_tables(n_pages * page + jnp.arange(t_new, dtype=jnp.int32), rope, tm_d)

    xp = x_prompt.reshape(batch * seq, d)
    xs = x_sample.reshape(bd * t_new, d)
    mem = mem_prompt.reshape(batch * n_mem, d)
    fg = final_norm_g[None, :]
    cmk = cache_mem_k.reshape(depth * bd * n_mem, x_heads * x_hd)
    cmv = cache_mem_v.reshape(depth * bd * n_mem, x_heads * x_hd)

    outs = [[] for _ in range(10)]
    yp = ys = None
    for l in range(depth):
        lw = _layer_weights(l, ln_g, w_in, q_norm_g, kv_norm_g, w_q_b, w_kv_b, conv_w, conv_b, w_r, b_r, w_i, b_i,
                            rg_lambda, w_out, rope, x_heads)
        n_heads = lw["n_heads"]
        last = l == depth - 1

        mk, mv = _mem_kv(mem, mem_norm_g[l][None, :], w_mem_kv[l].astype(BF16), tm=_pick(batch * n_mem, 512))
        qf, c_p, kpe_p, kn, kpb, v, sga, xrg, sgr, qx, sgx = _in_proj(xp, lw, cos_p, sin_p, decode=False, tm=tm_p,
                                                                      seq=seq)
        y_a = _attn_prompt(qf, kn, kpb, v, sga, batch=batch, seq=seq, n_heads=n_heads, tq=tq,
                           hp=4 if n_heads % 4 == 0 else 1)
        y_b, h_p, cv_p = _rg_prompt(xrg, sgr, lw, batch=batch, seq=seq, tc=tc)
        y_c = _xattn(qx, mk, mv, sgx, batch=batch, seq=seq, n_mem=n_mem, n_heads=x_heads, tq=_pick(seq, 2048))
        xp, yp = _out_proj([y_a], y_b, y_c, xp, lw, fg if last else None, decode=False, tm=tm_p)

        qc, qpe, c_s, kpe_s, sga, xrg, sgr, qx, sgx = _in_proj(xs, lw, cos_d, sin_d, decode=True, tm=tm_d, seq=t_new)
        o_lat = _attn_decode(page_table, qc, qpe, c_s, kpe_s, cache_ckv, cache_kpe_t, layer=l, t_new=t_new,
                             n_heads=n_heads, ch=ch)
        rgw = xrg.shape[1]
        to_tm = lambda a: jnp.transpose(a.reshape(bd, -1, rgw), (1, 0, 2))
        y_tm, h_s, cv_tm = _rg_decode(to_tm(xrg), to_tm(sgr), jnp.transpose(state_rg_conv[l], (1, 0, 2)),
                                      state_rg_h[l], lw)
        y_b = jnp.transpose(y_tm, (1, 0, 2)).reshape(bd * t_new, rgw)
        y_c = _xattn_decode(qx, cache_mem_k, cache_mem_v, sgx, layer=l, t_new=t_new, bb=_pick(bd, 4))
        xs, ys = _out_proj([o_lat, sga, lw["w_uvh"]], y_b, y_c, xs, lw, fg if last else None, decode=True, tm=tm_d)

        for lst, val in zip(outs, (
                c_p.reshape(batch, seq, -1), kpe_p.reshape(batch, seq, -1), h_p[:, 0], cv_p[:, SUBLANES - (n_tap - 1):],
                mk.reshape(batch, n_mem, x_heads, x_hd), mv.reshape(batch, n_mem, x_heads, x_hd),
                c_s.reshape(bd, t_new, -1), kpe_s.reshape(bd, t_new, -1), h_s, jnp.transpose(cv_tm, (1, 0, 2)))):
            lst.append(val)

    return (yp.reshape(batch, seq, d), ys.reshape(bd, t_new, d)) + tuple(jnp.stack(o) for o in outs)
```

```python
import functools

import jax
import jax.numpy as jnp
from jax import lax
from jax.experimental import pallas as pl
from jax.experimental.pallas import tpu as pltpu

EPS = 1e-6
ROPE_BASE = 10000.0
RG_C = 8.0
LANES = 128
SUBLANES = 8
NEG_BIG = -1e30
VMEM_LIMIT = 56 * 1024 * 1024

F32 = jnp.float32
BF16 = jnp.bfloat16

_NT = (((1,), (1,)), ((), ()))


def _params(n_grid, vmem=VMEM_LIMIT):
    return pltpu.CompilerParams(dimension_semantics=("arbitrary",) * n_grid, vmem_limit_bytes=vmem)


def _full(shape):
    n = len(shape)
    return pl.BlockSpec(shape, lambda *_: (0,) * n)


def _rms(x, g):
    return x * lax.rsqrt(jnp.mean(x * x, axis=-1, keepdims=True) + EPS) * g


def _sigmoid(x):
    return 1.0 / (1.0 + jnp.exp(-x))


def _silu(x):
    return x * _sigmoid(x)


def _dot(a, b):
    return jnp.dot(a, b, preferred_element_type=F32)


def _dot_nt(a, b):
    return lax.dot_general(a, b, _NT, preferred_element_type=F32)


def _rope_padded(x, cos_t, sin_t, half):
    swapped = pltpu.roll(x, half, 1) + pltpu.roll(x, LANES - half, 1)
    return x * cos_t + swapped * sin_t


def _in_proj_kernel(*refs, decode, n_heads, nope, half, scale, x_scale):
    (x_ref, lng_ref, cos_ref, sin_ref, wq_ref, qg_ref, wqb_ref, wc_ref, cg_ref, wk_ref,
     wa_ref, wxr_ref, wgr_ref, wqx_ref, wgx_ref) = refs[:15]
    if decode:
        wukt_ref = refs[15]
        (qc_ref, qpe_ref, c_ref, kpe_ref, sga_ref, xrg_ref, sgr_ref, qx_ref, sgx_ref) = refs[16:]
    else:
        wuk_ref, wuv_ref = refs[15:17]
        (qf_ref, c_ref, kpe_ref, kn_ref, kpb_ref, v_ref, sga_ref, xrg_ref, sgr_ref, qx_ref,
         sgx_ref) = refs[17:]

    u = _rms(x_ref[...], lng_ref[...]).astype(BF16)
    cos_t = cos_ref[...]
    sin_t = sin_ref[...]

    qn = _rms(_dot(u, wq_ref[...]), qg_ref[...]).astype(BF16)
    hw = nope + LANES
    for h in range(n_heads):
        qh = _dot(qn, wqb_ref[:, h * hw:(h + 1) * hw])
        q_nope = qh[:, :nope] * scale
        q_pe = _rope_padded(qh[:, nope:], cos_t, sin_t, half) * scale
        if decode:
            qc_ref[:, h * wukt_ref.shape[2]:(h + 1) * wukt_ref.shape[2]] = _dot(q_nope.astype(BF16), wukt_ref[h])
            qpe_ref[:, h * LANES:(h + 1) * LANES] = q_pe
        else:
            qf_ref[:, h * hw:h * hw + nope] = q_nope.astype(BF16)
            qf_ref[:, h * hw + nope:(h + 1) * hw] = q_pe.astype(BF16)

    c_new = _rms(_dot(u, wc_ref[...]), cg_ref[...])
    c_ref[...] = c_new
    kpe = _rope_padded(_dot(u, wk_ref[...]), cos_t, sin_t, half)
    kpe_ref[...] = kpe[:, :kpe_ref.shape[1]]
    if not decode:
        cb = c_new.astype(BF16)
        kn_ref[...] = _dot(cb, wuk_ref[...]).astype(BF16)
        v_ref[...] = _dot(cb, wuv_ref[...]).astype(BF16)
        kpb_ref[...] = kpe.astype(BF16)

    sga_ref[...] = _silu(_dot(u, wa_ref[...])).astype(sga_ref.dtype)
    xrg_ref[...] = _dot(u, wxr_ref[...])
    sgr_ref[...] = _silu(_dot(u, wgr_ref[...])).astype(sgr_ref.dtype)
    qx_ref[...] = (_dot(u, wqx_ref[...]) * x_scale).astype(qx_ref.dtype)
    sgx_ref[...] = _silu(_dot(u, wgx_ref[...])).astype(sgx_ref.dtype)


def _in_proj(x, lw, cos_t, sin_t, *, decode, tm, seq):
    n, d = x.shape
    n_heads, nope, rank, rope = lw["n_heads"], lw["nope"], lw["kv_rank"], lw["rope"]
    dv = lw["v_head"]
    rgw, xw = lw["w_xrg"].shape[1], lw["w_qx"].shape[1]
    act = F32 if decode else BF16
    tab_rows = cos_t.shape[0]
    n_tab = tab_rows // tm

    def row(width):
        return pl.BlockSpec((tm, width), lambda i: (i, 0))

    tab = pl.BlockSpec((tm, LANES), lambda i: (i % n_tab, 0))
    weights = [lw["w_qlat"], lw["q_norm_g"], lw["w_qb"], lw["w_c"], lw["kv_norm_g"], lw["w_kpe"],
               lw["w_ga"], lw["w_xrg"], lw["w_grg"], lw["w_qx"], lw["w_gx"]]
    if decode:
        weights += [lw["w_ukt"]]
        out_w = [(n_heads * rank, F32), (n_heads * LANES, F32), (rank, F32), (rope, F32), (n_heads * dv, act),
                 (rgw, F32), (rgw, act), (xw, act), (xw, act)]
    else:
        weights += [lw["w_uk"], lw["w_uv"]]
        out_w = [(n_heads * (nope + LANES), BF16), (rank, F32), (rope, F32), (n_heads * nope, BF16),
                 (LANES, BF16), (n_heads * dv, BF16), (n_heads * dv, act), (rgw, F32), (rgw, act), (xw, act),
                 (xw, act)]
    kern = functools.partial(_in_proj_kernel, decode=decode, n_heads=n_heads, nope=nope, half=rope // 2,
                             scale=float((nope + rope) ** -0.5), x_scale=float(lw["x_head_dim"] ** -0.5))
    return pl.pallas_call(
        kern,
        grid=(n // tm,),
        in_specs=[row(d), _full(lw["ln_g"].shape), tab, tab] + [_full(w.shape) for w in weights],
        out_specs=[row(w) for w, _ in out_w],
        out_shape=[jax.ShapeDtypeStruct((n, w), dt) for w, dt in out_w],
        compiler_params=_params(1),
        name="in_proj_dec" if decode else "in_proj",
    )(x, lw["ln_g"], cos_t, sin_t, *weights)


def _mem_kv_kernel(m_ref, g_ref, w_ref, k_ref, v_ref):
    u = _rms(m_ref[...], g_ref[...]).astype(BF16)
    kv = _dot(u, w_ref[...])
    xw = k_ref.shape[1]
    k_ref[...] = kv[:, :xw]
    v_ref[...] = kv[:, xw:]


def _mem_kv(mem, g, w, *, tm):
    n, d = mem.shape
    xw = w.shape[1] // 2
    return pl.pallas_call(
        _mem_kv_kernel,
        grid=(n // tm,),
        in_specs=[pl.BlockSpec((tm, d), lambda i: (i, 0)), _full(g.shape), _full(w.shape)],
        out_specs=[pl.BlockSpec((tm, xw), lambda i: (i, 0))] * 2,
        out_shape=[jax.ShapeDtypeStruct((n, xw), F32)] * 2,
        compiler_params=_params(1),
        name="mem_kv",
    )(mem, g, w)


def _attn_prompt_kernel(q_ref, kn_ref, kp_ref, v_ref, g_ref, o_ref, *, tq, hp, nsub):
    qi = pl.program_id(2)
    qw = q_ref.shape[1] // hp
    kw = kn_ref.shape[1] // hp
    dv = v_ref.shape[1] // hp

    def scores(j, h, rows=slice(None), width=tq):
        off = pl.multiple_of(j * tq, tq)
        k = jnp.concatenate([kn_ref[pl.ds(off, width), h * kw:(h + 1) * kw], kp_ref[pl.ds(off, width), :]], axis=1)
        return _dot_nt(q_ref[rows, h * qw:(h + 1) * qw], k), v_ref[pl.ds(off, width), h * dv:(h + 1) * dv]

    def update(s, v, carry):
        m, l, acc = carry
        m_new = jnp.maximum(m, jnp.max(s, axis=-1, keepdims=True))
        alpha = jnp.exp(m - m_new)
        p = jnp.exp(s - m_new)
        l = alpha * l + jnp.sum(p, axis=-1, keepdims=True)
        acc = alpha * acc + _dot(p.astype(BF16), v)
        return m_new, l, acc

    def body(j, carry):
        return tuple(update(*scores(j, h), carry[h]) for h in range(hp))

    init = tuple((jnp.full((tq, 1), NEG_BIG, F32), jnp.zeros((tq, 1), F32), jnp.zeros((tq, dv), F32))
                 for _ in range(hp))
    carry = lax.fori_loop(0, qi, body, init)
    rs = tq // nsub
    for h in range(hp):
        for r in range(nsub):
            rows, width = slice(r * rs, (r + 1) * rs), (r + 1) * rs
            s, v = scores(qi, h, rows, width)
            row = lax.broadcasted_iota(jnp.int32, (rs, width), 0) + r * rs
            col = lax.broadcasted_iota(jnp.int32, (rs, width), 1)
            _, l, acc = update(jnp.where(col <= row, s, NEG_BIG), v, tuple(c[rows] for c in carry[h]))
            sl = slice(h * dv, (h + 1) * dv)
            o_ref[rows, sl] = (acc / l * g_ref[rows, sl].astype(F32)).astype(o_ref.dtype)


def _attn_prompt(qf, kn, kpb, v, sga, *, batch, seq, n_heads, tq, hp, nsub):
    n = qf.shape[0]
    qw = qf.shape[1] // n_heads * hp
    kw = kn.shape[1] // n_heads * hp
    dv = v.shape[1] // n_heads * hp
    nq = seq // tq
    return pl.pallas_call(
        functools.partial(_attn_prompt_kernel, tq=tq, hp=hp, nsub=nsub),
        grid=(batch, n_heads // hp, nq),
        in_specs=[
            pl.BlockSpec((tq, qw), lambda b, h, i: (b * nq + i, h)),
            pl.BlockSpec((seq, kw), lambda b, h, i: (b, h)),
            pl.BlockSpec((seq, LANES), lambda b, h, i: (b, 0)),
            pl.BlockSpec((seq, dv), lambda b, h, i: (b, h)),
            pl.BlockSpec((tq, dv), lambda b, h, i: (b * nq + i, h)),
        ],
        out_specs=pl.BlockSpec((tq, dv), lambda b, h, i: (b * nq + i, h)),
        out_shape=jax.ShapeDtypeStruct((n, v.shape[1]), BF16),
        compiler_params=_params(3),
        name="attn_prompt",
    )(qf, kn, kpb, v, sga)


def _attn_decode_kernel(pt_ref, qc_ref, qpe_ref, cn_ref, kn_ref, ckv_hbm, kpe_hbm, o_ref,
                        kbuf, pbuf, sem, q_s, qp_s, m_s, l_s, acc_s, *, layer, ch, n_heads, n_split):
    b = pl.program_id(0)
    c = pl.program_id(1)
    nb = pl.num_programs(0)
    nc = pl.num_programs(1)
    g = b * nc + c
    slot = g % 2
    page = ckv_hbm.shape[2]
    t_new, rank = cn_ref.shape
    rope = kn_ref.shape[1]

    def copies(bb, cc, sl):
        out = []
        for p in range(ch):
            pg = pt_ref[bb, cc * ch + p]
            out.append(pltpu.make_async_copy(ckv_hbm.at[layer, pg], kbuf.at[sl, pl.ds(p * page, page), :],
                                             sem.at[sl, 0]))
            out.append(pltpu.make_async_copy(kpe_hbm.at[layer, pg], pbuf.at[sl, :, pl.ds(p * page, page)],
                                             sem.at[sl, 1]))
        return out

    @pl.when(g == 0)
    def _():
        for cp in copies(b, c, slot):
            cp.start()

    @pl.when(g + 1 < nb * nc)
    def _():
        wrap = c + 1 == nc
        for cp in copies(jnp.where(wrap, b + 1, b), jnp.where(wrap, 0, c + 1), 1 - slot):
            cp.start()

    @pl.when(c == 0)
    def _():
        qc = qc_ref[...]
        qp = qpe_ref[...]
        q_s[...] = jnp.concatenate([qc[:, h * rank:(h + 1) * rank] for h in range(n_heads)], axis=0).astype(BF16)
        qp_s[...] = jnp.concatenate([qp[:, h * LANES:(h + 1) * LANES] for h in range(n_heads)], axis=0).astype(BF16)
        m_s[...] = jnp.full(m_s.shape, NEG_BIG, F32)
        l_s[...] = jnp.zeros(l_s.shape, F32)
        acc_s[...] = jnp.zeros(acc_s.shape, F32)

    for cp in copies(b, c, slot):
        cp.wait()

    q = q_s[...]
    qp = qp_s[...][:, :rope]

    def piece(s, kb):
        m = jnp.max(s, axis=-1, keepdims=True)
        p = jnp.exp(s - m)
        return m, jnp.sum(p, axis=-1, keepdims=True), _dot(p.astype(BF16), kb)

    def update(parts):
        m_new = m_s[...]
        for m, _, _ in parts:
            m_new = jnp.maximum(m_new, m)
        alpha = jnp.exp(m_s[...] - m_new)
        l, acc = alpha * l_s[...], alpha * acc_s[...]
        for m, l_p, acc_p in parts:
            w = jnp.exp(m - m_new)
            l, acc = l + w * l_p, acc + w * acc_p
        m_s[...], l_s[...], acc_s[...] = m_new, l, acc

    keys = ch * page
    parts = []
    for lo in range(0, keys, keys // n_split):
        kb = kbuf[slot, lo:lo + keys // n_split, :].astype(BF16)
        pb = pbuf[slot, :, lo:lo + keys // n_split].astype(BF16)
        parts.append(piece(_dot_nt(q, kb) + _dot(qp, pb), kb))
    update(parts)

    @pl.when(c == nc - 1)
    def _():
        pad = LANES - t_new
        kb2 = jnp.concatenate([cn_ref[...], jnp.zeros((pad, rank), F32)], axis=0).astype(BF16)
        pb2 = jnp.concatenate([kn_ref[...], jnp.zeros((pad, rope), F32)], axis=0).astype(BF16)
        s = _dot_nt(q, kb2) + _dot_nt(qp, pb2)
        rows = s.shape[0]
        row = lax.broadcasted_iota(jnp.int32, (rows, LANES), 0)
        col = lax.broadcasted_iota(jnp.int32, (rows, LANES), 1)
        s = jnp.where(col <= row % t_new, s, NEG_BIG)
        update([piece(s, kb2)])
        o = acc_s[...] / l_s[...]
        for h in range(n_heads):
            o_ref[:, h * rank:(h + 1) * rank] = o[h * t_new:(h + 1) * t_new, :]


def _attn_decode(page_table, qc, qpe, c_new, kpe_new, cache_ckv, cache_kpe_t, *, layer, t_new, n_heads, ch):
    n = qc.shape[0]
    batch = n // t_new
    n_pages = page_table.shape[1]
    page, rank = cache_ckv.shape[2:]
    rope = cache_kpe_t.shape[2]
    nc = n_pages // ch
    rows = n_heads * t_new

    def per_batch(width):
        return pl.BlockSpec((t_new, width), lambda b, c, pt: (b, 0))

    grid_spec = pltpu.PrefetchScalarGridSpec(
        num_scalar_prefetch=1,
        grid=(batch, nc),
        in_specs=[per_batch(qc.shape[1]), per_batch(qpe.shape[1]), per_batch(rank), per_batch(rope),
                  pl.BlockSpec(memory_space=pl.ANY), pl.BlockSpec(memory_space=pl.ANY)],
        out_specs=per_batch(n_heads * rank),
        scratch_shapes=[
            pltpu.VMEM((2, ch * page, rank), F32),
            pltpu.VMEM((2, rope, ch * page), F32),
            pltpu.SemaphoreType.DMA((2, 2)),
            pltpu.VMEM((rows, rank), BF16),
            pltpu.VMEM((rows, LANES), BF16),
            pltpu.VMEM((rows, 1), F32),
            pltpu.VMEM((rows, 1), F32),
            pltpu.VMEM((rows, rank), F32),
        ],
    )
    return pl.pallas_call(
        functools.partial(_attn_decode_kernel, layer=layer, ch=ch, n_heads=n_heads, n_split=2 if ch % 2 == 0 else 1),
        grid_spec=grid_spec,
        out_shape=jax.ShapeDtypeStruct((n, n_heads * rank), F32),
        compiler_params=_params(2),
        name="attn_decode",
    )(page_table, qc, qpe, c_new, kpe_new, cache_ckv, cache_kpe_t)


def _rg_gates(xc, wr, br, wi, bi, lam):
    xb = xc.astype(BF16)
    r = _sigmoid(_dot(xb, wr) + br)
    ig = _sigmoid(_dot(xb, wi) + bi)
    z = -lam
    softplus = jnp.maximum(z, 0.0) + jnp.log1p(jnp.exp(-jnp.abs(z)))
    log_a = -RG_C * r * softplus
    a = jnp.exp(log_a)
    bx = jnp.sqrt(jnp.tanh(-log_a) * (1.0 + a * a)) * (ig * xc)
    return a, bx


def _rg_prompt_kernel(x_ref, halo_ref, g_ref, cw_ref, cb_ref, wr_ref, br_ref, wi_ref, bi_ref, lam_ref,
                      y_ref, hl_ref, cn_ref, xs, a_s, b_s, h_s, hc_s, *, tc):
    i = pl.program_id(1)
    width = x_ref.shape[1]
    n_tap = cw_ref.shape[0]
    x = x_ref[...]
    xs[SUBLANES:SUBLANES + tc, :] = x

    @pl.when(i == 0)
    def _():
        xs[0:SUBLANES, :] = jnp.zeros((SUBLANES, width), F32)
        hc_s[...] = jnp.zeros(hc_s.shape, F32)

    @pl.when(i > 0)
    def _():
        xs[0:SUBLANES, :] = halo_ref[...]

    xc = cb_ref[...]
    for k in range(n_tap):
        start = SUBLANES - (n_tap - 1) + k
        xc = xc + xs[start:start + tc, :] * cw_ref[k:k + 1, :]
    a, bx = _rg_gates(xc, wr_ref[...], br_ref[...], wi_ref[...], bi_ref[...], lam_ref[...])
    a_s[...] = a
    b_s[...] = bx

    row = lax.broadcasted_iota(jnp.int32, (SUBLANES, width), 0)

    def group(gi, h_prev):
        off = pl.multiple_of(gi * SUBLANES, SUBLANES)
        ag = a_s[pl.ds(off, SUBLANES), :]
        bg = b_s[pl.ds(off, SUBLANES), :]
        d = 1
        while d < SUBLANES:
            keep = row >= d
            bg = bg + ag * jnp.where(keep, pltpu.roll(bg, d, 0), 0.0)
            ag = ag * jnp.where(keep, pltpu.roll(ag, d, 0), 1.0)
            d *= 2
        h = bg + ag * h_prev
        h_s[pl.ds(off, SUBLANES), :] = h
        return jnp.broadcast_to(h[SUBLANES - 1:SUBLANES, :], (SUBLANES, width))

    h_last = lax.fori_loop(0, tc // SUBLANES, group, hc_s[...], unroll=2)
    hc_s[...] = h_last
    y_ref[...] = (h_s[...] * g_ref[...].astype(F32)).astype(y_ref.dtype)
    hl_ref[0] = h_last[0:1, :]
    cn_ref[0] = x[tc - SUBLANES:, :]


def _rg_prompt(x_rg, sg_rg, lw, *, batch, seq, tc):
    n, width = x_rg.shape
    nc = seq // tc
    per8 = tc // SUBLANES
    weights = [lw["conv_w"], lw["conv_b"], lw["w_r"], lw["b_r"], lw["w_i"], lw["b_i"], lw["rg_lambda"]]
    return pl.pallas_call(
        functools.partial(_rg_prompt_kernel, tc=tc),
        grid=(batch, nc),
        in_specs=[
            pl.BlockSpec((tc, width), lambda b, i: (b * nc + i, 0)),
            pl.BlockSpec((SUBLANES, width), lambda b, i: (jnp.maximum((b * nc + i) * per8 - 1, 0), 0)),
            pl.BlockSpec((tc, width), lambda b, i: (b * nc + i, 0)),
        ] + [_full(w.shape) for w in weights],
        out_specs=[
            pl.BlockSpec((tc, width), lambda b, i: (b * nc + i, 0)),
            pl.BlockSpec((1, 1, width), lambda b, i: (b, 0, 0)),
            pl.BlockSpec((1, SUBLANES, width), lambda b, i: (b, 0, 0)),
        ],
        out_shape=[
            jax.ShapeDtypeStruct((n, width), BF16),
            jax.ShapeDtypeStruct((batch, 1, width), F32),
            jax.ShapeDtypeStruct((batch, SUBLANES, width), F32),
        ],
        scratch_shapes=[
            pltpu.VMEM((tc + SUBLANES, width), F32),
            pltpu.VMEM((tc, width), F32),
            pltpu.VMEM((tc, width), F32),
            pltpu.VMEM((tc, width), F32),
            pltpu.VMEM((SUBLANES, width), F32),
        ],
        compiler_params=_params(2),
        name="rg_prompt",
    )(x_rg, x_rg, sg_rg, *weights)


def _rg_decode_kernel(x_ref, g_ref, c0_ref, h0_ref, cw_ref, cb_ref, wr_ref, br_ref, wi_ref, bi_ref, lam_ref,
                      y_ref, hl_ref, cn_ref):
    t_new = x_ref.shape[0]
    n_tap = cw_ref.shape[0]
    xp = [c0_ref[j] for j in range(n_tap - 1)] + [x_ref[t] for t in range(t_new)]
    h = h0_ref[...]
    for t in range(t_new):
        xc = cb_ref[...]
        for k in range(n_tap):
            xc = xc + xp[t + k] * cw_ref[k:k + 1, :]
        a, bx = _rg_gates(xc, wr_ref[...], br_ref[...], wi_ref[...], bi_ref[...], lam_ref[...])
        h = a * h + bx
        y_ref[t] = h * g_ref[t]
    hl_ref[...] = h
    for j in range(n_tap - 1):
        cn_ref[j] = xp[t_new + j]


def _rg_decode(x_tm, g_tm, conv0_tm, h0, lw):
    weights = [lw["conv_w"], lw["conv_b"], lw["w_r"], lw["b_r"], lw["w_i"], lw["b_i"], lw["rg_lambda"]]
    args = [x_tm, g_tm, conv0_tm, h0] + weights
    return pl.pallas_call(
        _rg_decode_kernel,
        in_specs=[_full(a.shape) for a in args],
        out_specs=[_full(x_tm.shape), _full(h0.shape), _full(conv0_tm.shape)],
        out_shape=[jax.ShapeDtypeStruct(x_tm.shape, F32), jax.ShapeDtypeStruct(h0.shape, F32),
                   jax.ShapeDtypeStruct(conv0_tm.shape, F32)],
        grid=(1,),
        compiler_params=_params(1),
        name="rg_decode",
    )(*args)


def _xattn_kernel(q_ref, k_ref, v_ref, g_ref, o_ref, *, n_heads):
    tq = q_ref.shape[0]
    hd = q_ref.shape[1] // n_heads
    min_rows = 2 * SUBLANES
    for h in range(n_heads):
        sl = slice(h * hd, (h + 1) * hd)
        q = q_ref[:, sl]
        if tq < min_rows:
            q = jnp.concatenate([q.astype(F32), jnp.zeros((min_rows - tq, hd), F32)], axis=0)
        k = k_ref[:, sl].astype(BF16)
        v = v_ref[:, sl].astype(BF16)
        s = _dot_nt(q.astype(BF16), k)
        p = jnp.exp(s - jnp.max(s, axis=-1, keepdims=True))
        o = _dot(p.astype(BF16), v) / jnp.sum(p, axis=-1, keepdims=True)
        o_ref[:, sl] = (o[:tq] * g_ref[:, sl].astype(F32)).astype(o_ref.dtype)


def _xattn_decode_kernel(q_ref, k_ref, v_ref, g_ref, o_ref, *, t_new, n_heads, bb):
    hd = k_ref.shape[1]
    kv_rows = k_ref.shape[0] // bb
    grp = 2 * SUBLANES
    pad = jnp.zeros((grp - t_new, hd), F32)
    row = lax.broadcasted_iota(jnp.int32, (n_heads * grp, kv_rows), 0)
    col = lax.broadcasted_iota(jnp.int32, (n_heads * grp, kv_rows), 1)
    own = col % n_heads == row // grp
    for j in range(bb):
        rows = slice(j * t_new, (j + 1) * t_new)
        k = k_ref[j * kv_rows:(j + 1) * kv_rows, :].astype(BF16)
        v = v_ref[j * kv_rows:(j + 1) * kv_rows, :].astype(BF16)
        pieces = []
        for h in range(n_heads):
            pieces += [q_ref[rows, h * hd:(h + 1) * hd], pad]
        q = jnp.concatenate(pieces, axis=0).astype(BF16)
        s = jnp.where(own, _dot_nt(q, k), NEG_BIG)
        p = jnp.exp(s - jnp.max(s, axis=-1, keepdims=True))
        o = _dot(p.astype(BF16), v) / jnp.sum(p, axis=-1, keepdims=True)
        for h in range(n_heads):
            sl = slice(h * hd, (h + 1) * hd)
            o_ref[rows, sl] = o[h * grp:h * grp + t_new] * g_ref[rows, sl]


def _xattn_decode(qx, cache_k, cache_v, sgx, *, layer, t_new, bb):
    n, xw = qx.shape
    _, bd, n_mem, n_heads, hd = cache_k.shape
    kv_rows = n_mem * n_heads
    rows = pl.BlockSpec((bb * t_new, xw), lambda i: (i, 0))
    mem = pl.BlockSpec((bb * kv_rows, hd), lambda i: (layer * (bd // bb) + i, 0))
    return pl.pallas_call(
        functools.partial(_xattn_decode_kernel, t_new=t_new, n_heads=n_heads, bb=bb),
        grid=(bd // bb,),
        in_specs=[rows, mem, mem, rows],
        out_specs=rows,
        out_shape=jax.ShapeDtypeStruct((n, xw), F32),
        compiler_params=_params(1),
        name="xattn_dec",
    )(qx, cache_k.reshape(-1, hd), cache_v.reshape(-1, hd), sgx)


def _xattn(qx, mem_k, mem_v, sgx, *, batch, seq, n_mem, n_heads, tq, base=0):
    n, xw = qx.shape
    nq = seq // tq
    return pl.pallas_call(
        functools.partial(_xattn_kernel, n_heads=n_heads),
        grid=(batch, nq),
        in_specs=[
            pl.BlockSpec((tq, xw), lambda b, i: (b * nq + i, 0)),
            pl.BlockSpec((n_mem, xw), lambda b, i: (base + b, 0)),
            pl.BlockSpec((n_mem, xw), lambda b, i: (base + b, 0)),
            pl.BlockSpec((tq, xw), lambda b, i: (b * nq + i, 0)),
        ],
        out_specs=pl.BlockSpec((tq, xw), lambda b, i: (b * nq + i, 0)),
        out_shape=jax.ShapeDtypeStruct((n, xw), qx.dtype),
        compiler_params=_params(2),
        name="xattn",
    )(qx, mem_k, mem_v, sgx)


def _out_proj_kernel(*refs, decode, final, n_heads):
    if decode:
        ol_ref, sga_ref, wuv_ref = refs[:3]
        refs = refs[3:]
    else:
        ya_ref = refs[0]
        refs = refs[1:]
    yb_ref, yc_ref, x_ref, woa_ref, wob_ref, woc_ref = refs[:6]
    refs = refs[6:]
    if final:
        fg_ref, xo_ref, yo_ref = refs
    else:
        (xo_ref,) = refs

    acc = x_ref[...] + _dot(yb_ref[...].astype(BF16), wob_ref[...]) + _dot(yc_ref[...].astype(BF16), woc_ref[...])
    if decode:
        rank, dv = wuv_ref.shape[1:]
        for h in range(n_heads):
            o_a = _dot(ol_ref[:, h * rank:(h + 1) * rank].astype(BF16), wuv_ref[h])
            y_a = (o_a * sga_ref[:, h * dv:(h + 1) * dv]).astype(BF16)
            acc = acc + _dot(y_a, woa_ref[h * dv:(h + 1) * dv, :])
    else:
        acc = acc + _dot(ya_ref[...], woa_ref[...])
    xo_ref[...] = acc
    if final:
        yo_ref[...] = _rms(acc, fg_ref[...])


def _out_proj(first, y_b, y_c, x, lw, final_g, *, decode, tm):
    n, d = x.shape
    weights = [lw["w_oa"], lw["w_ob"], lw["w_oc"]]
    final = final_g is not None
    acts = list(first) + [y_b, y_c, x]
    if decode:
        in_specs = [pl.BlockSpec((tm, first[0].shape[1]), lambda i: (i, 0)),
                    pl.BlockSpec((tm, first[1].shape[1]), lambda i: (i, 0)), _full(first[2].shape)]
    else:
        in_specs = [pl.BlockSpec((tm, first[0].shape[1]), lambda i: (i, 0))]
    in_specs += [pl.BlockSpec((tm, a.shape[1]), lambda i: (i, 0)) for a in (y_b, y_c, x)]
    in_specs += [_full(w.shape) for w in weights]
    args = acts + weights
    n_out = 1
    if final:
        in_specs.append(_full(final_g.shape))
        args.append(final_g)
        n_out = 2
    out = pl.pallas_call(
        functools.partial(_out_proj_kernel, decode=decode, final=final, n_heads=lw["n_heads"]),
        grid=(n // tm,),
        in_specs=in_specs,
        out_specs=[pl.BlockSpec((tm, d), lambda i: (i, 0))] * n_out,
        out_shape=[jax.ShapeDtypeStruct((n, d), F32)] * n_out,
        compiler_params=_params(1),
        name="out_proj_dec" if decode else "out_proj",
    )(*args)
    return out if final else (out[0], None)


def _rope_tables(pos, rope, rows):
    half = rope // 2
    inv = 1.0 / (ROPE_BASE ** (jnp.arange(half, dtype=F32) / half))
    ang = pos.astype(F32)[:, None] * inv[None, :]
    cos, sin = jnp.cos(ang), jnp.sin(ang)
    zeros = jnp.zeros((pos.shape[0], LANES - rope), F32)
    cos_t = jnp.concatenate([cos, cos, zeros], axis=1)
    sin_t = jnp.concatenate([-sin, sin, zeros], axis=1)
    reps = max(rows // pos.shape[0], 1)
    return jnp.tile(cos_t, (reps, 1)), jnp.tile(sin_t, (reps, 1))


def _block_diag(w):
    nb, bs, _ = w.shape
    eye = jnp.eye(nb, dtype=w.dtype)
    return (w[:, :, None, :] * eye[:, None, :, None]).reshape(nb * bs, nb * bs)


def _layer_weights(l, ln_g, w_in, q_norm_g, kv_norm_g, w_q_b, w_kv_b, conv_w, conv_b, w_r, b_r, w_i, b_i,
                   rg_lambda, w_out, rope, x_heads):
    d = w_in.shape[1]
    q_rank, n_heads, qk = w_q_b.shape[1:]
    kv_rank = w_kv_b.shape[1]
    nope = qk - rope
    dv = w_kv_b.shape[3] - nope
    rgw = conv_w.shape[2]
    mla_w = n_heads * dv
    xw = (w_in.shape[2] - q_rank - kv_rank - rope - mla_w - 2 * rgw) // 2
    splits, off = [], 0
    for width in (q_rank, kv_rank, rope, mla_w, rgw, rgw, xw, xw):
        splits.append(w_in[l][:, off:off + width].astype(BF16))
        off += width
    w_qlat, w_c, w_kpe, w_ga, w_xrg, w_grg, w_qx, w_gx = splits
    w_qb = jnp.pad(w_q_b[l], ((0, 0), (0, 0), (0, LANES - rope))).reshape(q_rank, n_heads * (nope + LANES))
    wo = w_out[l].astype(BF16)
    return dict(
        n_heads=n_heads, nope=nope, rope=rope, kv_rank=kv_rank, v_head=dv, x_head_dim=xw // x_heads,
        ln_g=ln_g[l][None, :], w_qlat=w_qlat, q_norm_g=q_norm_g[l][None, :], w_qb=w_qb.astype(BF16),
        w_c=w_c, kv_norm_g=kv_norm_g[l][None, :], w_kpe=jnp.pad(w_kpe, ((0, 0), (0, LANES - rope))),
        w_ga=w_ga, w_xrg=w_xrg, w_grg=w_grg, w_qx=w_qx, w_gx=w_gx,
        w_uk=w_kv_b[l][:, :, :nope].reshape(kv_rank, n_heads * nope).astype(BF16),
        w_uv=w_kv_b[l][:, :, nope:].reshape(kv_rank, n_heads * dv).astype(BF16),
        w_ukt=jnp.transpose(w_kv_b[l][:, :, :nope], (1, 2, 0)).astype(BF16),
        w_uvh=jnp.transpose(w_kv_b[l][:, :, nope:], (1, 0, 2)).astype(BF16),
        conv_w=conv_w[l], conv_b=conv_b[l][None, :],
        w_r=_block_diag(w_r[l]).astype(BF16), b_r=b_r[l].reshape(1, rgw),
        w_i=_block_diag(w_i[l]).astype(BF16), b_i=b_i[l].reshape(1, rgw),
        rg_lambda=rg_lambda[l][None, :],
        w_oa=wo[:mla_w], w_ob=wo[mla_w:mla_w + rgw], w_oc=wo[mla_w + rgw:],
    )


def _pick(n, pref):
    t = min(n, pref)
    while n % t:
        t //= 2
    return t


def kernel(x_prompt, x_sample, mem_prompt, cache_ckv, cache_kpe, state_rg_h, state_rg_conv, cache_mem_k, cache_mem_v, page_table, ln_g, w_in, q_norm_g, kv_norm_g, w_q_b, w_kv_b, conv_w, conv_b, w_r, b_r, w_i, b_i, rg_lambda, mem_norm_g, w_mem_kv, w_out, final_norm_g):
    batch, seq, d = x_prompt.shape
    bd, t_new, _ = x_sample.shape
    depth = w_in.shape[0]
    n_mem = mem_prompt.shape[1]
    x_heads, x_hd = cache_mem_k.shape[3:]
    n_pages = page_table.shape[1]
    page = cache_ckv.shape[2]
    rope = cache_kpe.shape[3]
    n_tap = conv_w.shape[1]
    assert t_new >= n_tap - 1 and seq >= SUBLANES and t_new == SUBLANES

    tm_p = _pick(seq, 512)
    tm_d = _pick(bd * t_new, 256)
    tq = _pick(seq, 1024)
    tc = _pick(seq, 256)
    ch = _pick(n_pages, 64)
    cache_kpe_t = jnp.swapaxes(cache_kpe, 2, 3)

    cos_p, sin_p = _rope_tables(jnp.arange(seq, dtype=jnp.int32), rope, tm_p)
    cos_d, sin_d = _rope_tables(n_pages * page + jnp.arange(t_new, dtype=jnp.int32), rope, tm_d)

    xp = x_prompt.reshape(batch * seq, d)
    xs = x_sample.reshape(bd * t_new, d)
    mem = mem_prompt.reshape(batch * n_mem, d)
    fg = final_norm_g[None, :]
    cmk = cache_mem_k.reshape(depth * bd * n_mem, x_heads * x_hd)
    cmv = cache_mem_v.reshape(depth * bd * n_mem, x_heads * x_hd)

    outs = [[] for _ in range(10)]
    yp = ys = None
    for l in range(depth):
        lw = _layer_weights(l, ln_g, w_in, q_norm_g, kv_norm_g, w_q_b, w_kv_b, conv_w, conv_b, w_r, b_r, w_i, b_i,
                            rg_lambda, w_out, rope, x_heads)
        n_heads = lw["n_heads"]
        last = l == depth - 1

        mk, mv = _mem_kv(mem, mem_norm_g[l][None, :], w_mem_kv[l].astype(BF16), tm=_pick(batch * n_mem, 512))
        qf, c_p, kpe_p, kn, kpb, v, sga, xrg, sgr, qx, sgx = _in_proj(xp, lw, cos_p, sin_p, decode=False, tm=tm_p,
                                                                      seq=seq)
        y_a = _attn_prompt(qf, kn, kpb, v, sga, batch=batch, seq=seq, n_heads=n_heads, tq=tq,
                           hp=4 if n_heads % 4 == 0 else 1, nsub=2 if tq % (2 * LANES) == 0 else 1)
        y_b, h_p, cv_p = _rg_prompt(xrg, sgr, lw, batch=batch, seq=seq, tc=tc)
        y_c = _xattn(qx, mk, mv, sgx, batch=batch, seq=seq, n_mem=n_mem, n_heads=x_heads, tq=_pick(seq, 2048))
        xp, yp = _out_proj([y_a], y_b, y_c, xp, lw, fg if last else None, decode=False, tm=tm_p)

        qc, qpe, c_s, kpe_s, sga, xrg, sgr, qx, sgx = _in_proj(xs, lw, cos_d, sin_d, decode=True, tm=tm_d, seq=t_new)
        o_lat = _attn_decode(page_table, qc, qpe, c_s, kpe_s, cache_ckv, cache_kpe_t, layer=l, t_new=t_new,
                             n_heads=n_heads, ch=ch)
        rgw = xrg.shape[1]
        to_tm = lambda a: jnp.transpose(a.reshape(bd, -1, rgw), (1, 0, 2))
        y_tm, h_s, cv_tm = _rg_decode(to_tm(xrg), to_tm(sgr), jnp.transpose(state_rg_conv[l], (1, 0, 2)),
                                      state_rg_h[l], lw)
        y_b = jnp.transpose(y_tm, (1, 0, 2)).reshape(bd * t_new, rgw)
        y_c = _xattn_decode(qx, cache_mem_k, cache_mem_v, sgx, layer=l, t_new=t_new, bb=_pick(bd, 4))
        xs, ys = _out_proj([o_lat, sga, lw["w_uvh"]], y_b, y_c, xs, lw, fg if last else None, decode=True, tm=tm_d)

        for lst, val in zip(outs, (
                c_p.reshape(batch, seq, -1), kpe_p.reshape(batch, seq, -1), h_p[:, 0], cv_p[:, SUBLANES - (n_tap - 1):],
                mk.reshape(batch, n_mem, x_heads, x_hd), mv.reshape(batch, n_mem, x_heads, x_hd),
                c_s.reshape(bd, t_new, -1), kpe_s.reshape(bd, t_new, -1), h_s, jnp.transpose(cv_tm, (1, 0, 2)))):
            lst.append(val)

    return (yp.reshape(batch, seq, d), ys.reshape(bd, t_new, d)) + tuple(jnp.stack(o) for o in outs)
```

```python
import functools

import jax
import jax.numpy as jnp
from jax import lax
from jax.experimental import pallas as pl
from jax.experimental.pallas import tpu as pltpu

EPS = 1e-6
ROPE_BASE = 10000.0
RG_C = 8.0
LANES = 128
SUBLANES = 8
DMA_QUEUES = 2
NEG_BIG = -1e30
VMEM_LIMIT = 56 * 1024 * 1024

F32 = jnp.float32
BF16 = jnp.bfloat16

_NT = (((1,), (1,)), ((), ()))


def _params(n_grid, vmem=VMEM_LIMIT):
    return pltpu.CompilerParams(dimension_semantics=("arbitrary",) * n_grid, vmem_limit_bytes=vmem)


def _full(shape):
    n = len(shape)
    return pl.BlockSpec(shape, lambda *_: (0,) * n)


def _rms(x, g):
    return x * lax.rsqrt(jnp.mean(x * x, axis=-1, keepdims=True) + EPS) * g


def _sigmoid(x):
    return 1.0 / (1.0 + jnp.exp(-x))


def _silu(x):
    return x * _sigmoid(x)


def _dot(a, b):
    return jnp.dot(a, b, preferred_element_type=F32)


def _dot_nt(a, b):
    return lax.dot_general(a, b, _NT, preferred_element_type=F32)


def _rope_padded(x, cos_t, sin_t, half):
    swapped = pltpu.roll(x, half, 1) + pltpu.roll(x, LANES - half, 1)
    return x * cos_t + swapped * sin_t


def _in_proj_kernel(*refs, decode, n_heads, nope, half, scale, x_scale):
    (x_ref, lng_ref, cos_ref, sin_ref, wq_ref, qg_ref, wqb_ref, wc_ref, cg_ref, wk_ref,
     wa_ref, wxr_ref, wgr_ref, wqx_ref, wgx_ref) = refs[:15]
    if decode:
        wukt_ref = refs[15]
        (qc_ref, qpe_ref, c_ref, kpe_ref, sga_ref, xrg_ref, sgr_ref, qx_ref, sgx_ref) = refs[16:]
    else:
        wuk_ref, wuv_ref = refs[15:17]
        (qf_ref, c_ref, kpe_ref, kn_ref, kpb_ref, v_ref, sga_ref, xrg_ref, sgr_ref, qx_ref,
         sgx_ref) = refs[17:]

    u = _rms(x_ref[...], lng_ref[...]).astype(BF16)
    cos_t = cos_ref[...]
    sin_t = sin_ref[...]

    qn = _rms(_dot(u, wq_ref[...]), qg_ref[...]).astype(BF16)
    hw = nope + LANES
    for h in range(n_heads):
        qh = _dot(qn, wqb_ref[:, h * hw:(h + 1) * hw])
        q_nope = qh[:, :nope] * scale
        q_pe = _rope_padded(qh[:, nope:], cos_t, sin_t, half) * scale
        if decode:
            qc_ref[:, h * wukt_ref.shape[2]:(h + 1) * wukt_ref.shape[2]] = _dot(q_nope.astype(BF16), wukt_ref[h])
            qpe_ref[:, h * LANES:(h + 1) * LANES] = q_pe
        else:
            qf_ref[:, h * hw:h * hw + nope] = q_nope.astype(BF16)
            qf_ref[:, h * hw + nope:(h + 1) * hw] = q_pe.astype(BF16)

    c_new = _rms(_dot(u, wc_ref[...]), cg_ref[...])
    c_ref[...] = c_new
    kpe = _rope_padded(_dot(u, wk_ref[...]), cos_t, sin_t, half)
    kpe_ref[...] = kpe[:, :kpe_ref.shape[1]]
    if not decode:
        cb = c_new.astype(BF16)
        kn_ref[...] = _dot(cb, wuk_ref[...]).astype(BF16)
        v_ref[...] = _dot(cb, wuv_ref[...]).astype(BF16)
        kpb_ref[...] = kpe.astype(BF16)

    sga_ref[...] = _silu(_dot(u, wa_ref[...])).astype(sga_ref.dtype)
    xrg_ref[...] = _dot(u, wxr_ref[...])
    sgr_ref[...] = _silu(_dot(u, wgr_ref[...])).astype(sgr_ref.dtype)
    qx_ref[...] = (_dot(u, wqx_ref[...]) * x_scale).astype(qx_ref.dtype)
    sgx_ref[...] = _silu(_dot(u, wgx_ref[...])).astype(sgx_ref.dtype)


def _in_proj(x, lw, cos_t, sin_t, *, decode, tm, seq):
    n, d = x.shape
    n_heads, nope, rank, rope = lw["n_heads"], lw["nope"], lw["kv_rank"], lw["rope"]
    dv = lw["v_head"]
    rgw, xw = lw["w_xrg"].shape[1], lw["w_qx"].shape[1]
    act = F32 if decode else BF16
    tab_rows = cos_t.shape[0]
    n_tab = tab_rows // tm

    def row(width):
        return pl.BlockSpec((tm, width), lambda i: (i, 0))

    tab = pl.BlockSpec((tm, LANES), lambda i: (i % n_tab, 0))
    weights = [lw["w_qlat"], lw["q_norm_g"], lw["w_qb"], lw["w_c"], lw["kv_norm_g"], lw["w_kpe"],
               lw["w_ga"], lw["w_xrg"], lw["w_grg"], lw["w_qx"], lw["w_gx"]]
    if decode:
        weights += [lw["w_ukt"]]
        out_w = [(n_heads * rank, F32), (n_heads * LANES, F32), (rank, F32), (rope, F32), (n_heads * dv, act),
                 (rgw, F32), (rgw, act), (xw, act), (xw, act)]
    else:
        weights += [lw["w_uk"], lw["w_uv"]]
        out_w = [(n_heads * (nope + LANES), BF16), (rank, F32), (rope, F32), (n_heads * nope, BF16),
                 (LANES, BF16), (n_heads * dv, BF16), (n_heads * dv, act), (rgw, F32), (rgw, act), (xw, act),
                 (xw, act)]
    kern = functools.partial(_in_proj_kernel, decode=decode, n_heads=n_heads, nope=nope, half=rope // 2,
                             scale=float((nope + rope) ** -0.5), x_scale=float(lw["x_head_dim"] ** -0.5))
    return pl.pallas_call(
        kern,
        grid=(n // tm,),
        in_specs=[row(d), _full(lw["ln_g"].shape), tab, tab] + [_full(w.shape) for w in weights],
        out_specs=[row(w) for w, _ in out_w],
        out_shape=[jax.ShapeDtypeStruct((n, w), dt) for w, dt in out_w],
        compiler_params=_params(1),
        name="in_proj_dec" if decode else "in_proj",
    )(x, lw["ln_g"], cos_t, sin_t, *weights)


def _mem_kv_kernel(m_ref, g_ref, w_ref, k_ref, v_ref):
    u = _rms(m_ref[...], g_ref[...]).astype(BF16)
    kv = _dot(u, w_ref[...])
    xw = k_ref.shape[1]
    k_ref[...] = kv[:, :xw]
    v_ref[...] = kv[:, xw:]


def _mem_kv(mem, g, w, *, tm):
    n, d = mem.shape
    xw = w.shape[1] // 2
    return pl.pallas_call(
        _mem_kv_kernel,
        grid=(n // tm,),
        in_specs=[pl.BlockSpec((tm, d), lambda i: (i, 0)), _full(g.shape), _full(w.shape)],
        out_specs=[pl.BlockSpec((tm, xw), lambda i: (i, 0))] * 2,
        out_shape=[jax.ShapeDtypeStruct((n, xw), F32)] * 2,
        compiler_params=_params(1),
        name="mem_kv",
    )(mem, g, w)


def _attn_prompt_kernel(q_ref, kn_ref, kp_ref, v_ref, g_ref, o_ref, *, tq, hp, nsub):
    qi = pl.program_id(2)
    qw = q_ref.shape[1] // hp
    kw = kn_ref.shape[1] // hp
    dv = v_ref.shape[1] // hp

    def scores(j, h, rows=slice(None), width=tq):
        off = pl.multiple_of(j * tq, tq)
        k = jnp.concatenate([kn_ref[pl.ds(off, width), h * kw:(h + 1) * kw], kp_ref[pl.ds(off, width), :]], axis=1)
        return _dot_nt(q_ref[rows, h * qw:(h + 1) * qw], k), v_ref[pl.ds(off, width), h * dv:(h + 1) * dv]

    def update(s, v, carry):
        m, l, acc = carry
        m_new = jnp.maximum(m, jnp.max(s, axis=-1, keepdims=True))
        alpha = jnp.exp(m - m_new)
        p = jnp.exp(s - m_new)
        l = alpha * l + jnp.sum(p, axis=-1, keepdims=True)
        acc = alpha * acc + _dot(p.astype(BF16), v)
        return m_new, l, acc

    def body(j, carry):
        return tuple(update(*scores(j, h), carry[h]) for h in range(hp))

    init = tuple((jnp.full((tq, 1), NEG_BIG, F32), jnp.zeros((tq, 1), F32), jnp.zeros((tq, dv), F32))
                 for _ in range(hp))
    carry = lax.fori_loop(0, qi, body, init)
    rs = tq // nsub
    for h in range(hp):
        for r in range(nsub):
            rows, width = slice(r * rs, (r + 1) * rs), (r + 1) * rs
            s, v = scores(qi, h, rows, width)
            row = lax.broadcasted_iota(jnp.int32, (rs, width), 0) + r * rs
            col = lax.broadcasted_iota(jnp.int32, (rs, width), 1)
            _, l, acc = update(jnp.where(col <= row, s, NEG_BIG), v, tuple(c[rows] for c in carry[h]))
            sl = slice(h * dv, (h + 1) * dv)
            o_ref[rows, sl] = (acc / l * g_ref[rows, sl].astype(F32)).astype(o_ref.dtype)


def _attn_prompt(qf, kn, kpb, v, sga, *, batch, seq, n_heads, tq, hp, nsub):
    n = qf.shape[0]
    qw = qf.shape[1] // n_heads * hp
    kw = kn.shape[1] // n_heads * hp
    dv = v.shape[1] // n_heads * hp
    nq = seq // tq
    return pl.pallas_call(
        functools.partial(_attn_prompt_kernel, tq=tq, hp=hp, nsub=nsub),
        grid=(batch, n_heads // hp, nq),
        in_specs=[
            pl.BlockSpec((tq, qw), lambda b, h, i: (b * nq + i, h)),
            pl.BlockSpec((seq, kw), lambda b, h, i: (b, h)),
            pl.BlockSpec((seq, LANES), lambda b, h, i: (b, 0)),
            pl.BlockSpec((seq, dv), lambda b, h, i: (b, h)),
            pl.BlockSpec((tq, dv), lambda b, h, i: (b * nq + i, h)),
        ],
        out_specs=pl.BlockSpec((tq, dv), lambda b, h, i: (b * nq + i, h)),
        out_shape=jax.ShapeDtypeStruct((n, v.shape[1]), BF16),
        compiler_params=_params(3),
        name="attn_prompt",
    )(qf, kn, kpb, v, sga)


def _attn_decode_kernel(pt_ref, qc_ref, qpe_ref, cn_ref, kn_ref, ckv_hbm, kpe_hbm, o_ref,
                        kbuf, pbuf, sem, q_s, qp_s, m_s, l_s, acc_s, *, layer, ch, n_heads, n_split):
    b = pl.program_id(0)
    c = pl.program_id(1)
    nb = pl.num_programs(0)
    nc = pl.num_programs(1)
    g = b * nc + c
    slot = g % 2
    page = ckv_hbm.shape[2]
    t_new, rank = cn_ref.shape
    rope = kn_ref.shape[1]

    def copies(bb, cc, sl):
        out = []
        for p in range(ch):
            pg = pt_ref[bb, cc * ch + p]
            out.append(pltpu.make_async_copy(ckv_hbm.at[layer, pg], kbuf.at[sl, pl.ds(p * page, page), :],
                                             sem.at[sl, 0]))
            out.append(pltpu.make_async_copy(kpe_hbm.at[layer, pg], pbuf.at[sl, :, pl.ds(p * page, page)],
                                             sem.at[sl, 1]))
        return out

    def start_all(cps):
        for i, cp in enumerate(cps):
            cp.start(priority=(i // 2) % DMA_QUEUES)

    @pl.when(g == 0)
    def _():
        start_all(copies(b, c, slot))

    @pl.when(g + 1 < nb * nc)
    def _():
        wrap = c + 1 == nc
        start_all(copies(jnp.where(wrap, b + 1, b), jnp.where(wrap, 0, c + 1), 1 - slot))

    @pl.when(c == 0)
    def _():
        qc = qc_ref[...]
        qp = qpe_ref[...]
        q_s[...] = jnp.concatenate([qc[:, h * rank:(h + 1) * rank] for h in range(n_heads)], axis=0).astype(BF16)
        qp_s[...] = jnp.concatenate([qp[:, h * LANES:(h + 1) * LANES] for h in range(n_heads)], axis=0).astype(BF16)
        m_s[...] = jnp.full(m_s.shape, NEG_BIG, F32)
        l_s[...] = jnp.zeros(l_s.shape, F32)
        acc_s[...] = jnp.zeros(acc_s.shape, F32)

    for cp in copies(b, c, slot):
        cp.wait()

    q = q_s[...]
    qp = qp_s[...][:, :rope]

    def piece(s, kb):
        m = jnp.max(s, axis=-1, keepdims=True)
        p = jnp.exp(s - m)
        return m, jnp.sum(p, axis=-1, keepdims=True), _dot(p.astype(BF16), kb)

    def update(parts):
        m_new = m_s[...]
        for m, _, _ in parts:
            m_new = jnp.maximum(m_new, m)
        alpha = jnp.exp(m_s[...] - m_new)
        l, acc = alpha * l_s[...], alpha * acc_s[...]
        for m, l_p, acc_p in parts:
            w = jnp.exp(m - m_new)
            l, acc = l + w * l_p, acc + w * acc_p
        m_s[...], l_s[...], acc_s[...] = m_new, l, acc

    keys = ch * page
    parts = []
    for lo in range(0, keys, keys // n_split):
        kb = kbuf[slot, lo:lo + keys // n_split, :].astype(BF16)
        pb = pbuf[slot, :, lo:lo + keys // n_split].astype(BF16)
        parts.append(piece(_dot_nt(q, kb) + _dot(qp, pb), kb))
    update(parts)

    @pl.when(c == nc - 1)
    def _():
        pad = LANES - t_new
        kb2 = jnp.concatenate([cn_ref[...], jnp.zeros((pad, rank), F32)], axis=0).astype(BF16)
        pb2 = jnp.concatenate([kn_ref[...], jnp.zeros((pad, rope), F32)], axis=0).astype(BF16)
        s = _dot_nt(q, kb2) + _dot_nt(qp, pb2)
        rows = s.shape[0]
        row = lax.broadcasted_iota(jnp.int32, (rows, LANES), 0)
        col = lax.broadcasted_iota(jnp.int32, (rows, LANES), 1)
        s = jnp.where(col <= row % t_new, s, NEG_BIG)
        update([piece(s, kb2)])
        o = acc_s[...] / l_s[...]
        for h in range(n_heads):
            o_ref[:, h * rank:(h + 1) * rank] = o[h * t_new:(h + 1) * t_new, :]


def _attn_decode(page_table, qc, qpe, c_new, kpe_new, cache_ckv, cache_kpe_t, *, layer, t_new, n_heads, ch):
    n = qc.shape[0]
    batch = n // t_new
    n_pages = page_table.shape[1]
    page, rank = cache_ckv.shape[2:]
    rope = cache_kpe_t.shape[2]
    nc = n_pages // ch
    rows = n_heads * t_new

    def per_batch(width):
        return pl.BlockSpec((t_new, width), lambda b, c, pt: (b, 0))

    grid_spec = pltpu.PrefetchScalarGridSpec(
        num_scalar_prefetch=1,
        grid=(batch, nc),
        in_specs=[per_batch(qc.shape[1]), per_batch(qpe.shape[1]), per_batch(rank), per_batch(rope),
                  pl.BlockSpec(memory_space=pl.ANY), pl.BlockSpec(memory_space=pl.ANY)],
        out_specs=per_batch(n_heads * rank),
        scratch_shapes=[
            pltpu.VMEM((2, ch * page, rank), F32),
            pltpu.VMEM((2, rope, ch * page), F32),
            pltpu.SemaphoreType.DMA((2, 2)),
            pltpu.VMEM((rows, rank), BF16),
            pltpu.VMEM((rows, LANES), BF16),
            pltpu.VMEM((rows, 1), F32),
            pltpu.VMEM((rows, 1), F32),
            pltpu.VMEM((rows, rank), F32),
        ],
    )
    return pl.pallas_call(
        functools.partial(_attn_decode_kernel, layer=layer, ch=ch, n_heads=n_heads, n_split=2 if ch % 2 == 0 else 1),
        grid_spec=grid_spec,
        out_shape=jax.ShapeDtypeStruct((n, n_heads * rank), F32),
        compiler_params=_params(2),
        name="attn_decode",
    )(page_table, qc, qpe, c_new, kpe_new, cache_ckv, cache_kpe_t)


def _rg_gates(xc, wr, br, wi, bi, lam):
    xb = xc.astype(BF16)
    r = _sigmoid(_dot(xb, wr) + br)
    ig = _sigmoid(_dot(xb, wi) + bi)
    z = -lam
    softplus = jnp.maximum(z, 0.0) + jnp.log1p(jnp.exp(-jnp.abs(z)))
    log_a = -RG_C * r * softplus
    a = jnp.exp(log_a)
    bx = jnp.sqrt(jnp.tanh(-log_a) * (1.0 + a * a)) * (ig * xc)
    return a, bx


def _rg_prompt_kernel(x_ref, halo_ref, g_ref, cw_ref, cb_ref, wr_ref, br_ref, wi_ref, bi_ref, lam_ref,
                      y_ref, hl_ref, cn_ref, xs, a_s, b_s, h_s, hc_s, *, tc):
    i = pl.program_id(1)
    width = x_ref.shape[1]
    n_tap = cw_ref.shape[0]
    x = x_ref[...]
    xs[SUBLANES:SUBLANES + tc, :] = x

    @pl.when(i == 0)
    def _():
        xs[0:SUBLANES, :] = jnp.zeros((SUBLANES, width), F32)
        hc_s[...] = jnp.zeros(hc_s.shape, F32)

    @pl.when(i > 0)
    def _():
        xs[0:SUBLANES, :] = halo_ref[...]

    xc = cb_ref[...]
    for k in range(n_tap):
        start = SUBLANES - (n_tap - 1) + k
        xc = xc + xs[start:start + tc, :] * cw_ref[k:k + 1, :]
    a, bx = _rg_gates(xc, wr_ref[...], br_ref[...], wi_ref[...], bi_ref[...], lam_ref[...])
    a_s[...] = a
    b_s[...] = bx

    row = lax.broadcasted_iota(jnp.int32, (SUBLANES, width), 0)

    def group(gi, h_prev):
        off = pl.multiple_of(gi * SUBLANES, SUBLANES)
        ag = a_s[pl.ds(off, SUBLANES), :]
        bg = b_s[pl.ds(off, SUBLANES), :]
        d = 1
        while d < SUBLANES:
            keep = row >= d
            bg = bg + ag * jnp.where(keep, pltpu.roll(bg, d, 0), 0.0)
            ag = ag * jnp.where(keep, pltpu.roll(ag, d, 0), 1.0)
            d *= 2
        h = bg + ag * h_prev
        h_s[pl.ds(off, SUBLANES), :] = h
        return jnp.broadcast_to(h[SUBLANES - 1:SUBLANES, :], (SUBLANES, width))

    h_last = lax.fori_loop(0, tc // SUBLANES, group, hc_s[...], unroll=2)
    hc_s[...] = h_last
    y_ref[...] = (h_s[...] * g_ref[...].astype(F32)).astype(y_ref.dtype)
    hl_ref[0] = h_last[0:1, :]
    cn_ref[0] = x[tc - SUBLANES:, :]


def _rg_prompt(x_rg, sg_rg, lw, *, batch, seq, tc):
    n, width = x_rg.shape
    nc = seq // tc
    per8 = tc // SUBLANES
    weights = [lw["conv_w"], lw["conv_b"], lw["w_r"], lw["b_r"], lw["w_i"], lw["b_i"], lw["rg_lambda"]]
    return pl.pallas_call(
        functools.partial(_rg_prompt_kernel, tc=tc),
        grid=(batch, nc),
        in_specs=[
            pl.BlockSpec((tc, width), lambda b, i: (b * nc + i, 0)),
            pl.BlockSpec((SUBLANES, width), lambda b, i: (jnp.maximum((b * nc + i) * per8 - 1, 0), 0)),
            pl.BlockSpec((tc, width), lambda b, i: (b * nc + i, 0)),
        ] + [_full(w.shape) for w in weights],
        out_specs=[
            pl.BlockSpec((tc, width), lambda b, i: (b * nc + i, 0)),
            pl.BlockSpec((1, 1, width), lambda b, i: (b, 0, 0)),
            pl.BlockSpec((1, SUBLANES, width), lambda b, i: (b, 0, 0)),
        ],
        out_shape=[
            jax.ShapeDtypeStruct((n, width), BF16),
            jax.ShapeDtypeStruct((batch, 1, width), F32),
            jax.ShapeDtypeStruct((batch, SUBLANES, width), F32),
        ],
        scratch_shapes=[
            pltpu.VMEM((tc + SUBLANES, width), F32),
            pltpu.VMEM((tc, width), F32),
            pltpu.VMEM((tc, width), F32),
            pltpu.VMEM((tc, width), F32),
            pltpu.VMEM((SUBLANES, width), F32),
        ],
        compiler_params=_params(2),
        name="rg_prompt",
    )(x_rg, x_rg, sg_rg, *weights)


def _rg_decode_kernel(x_ref, g_ref, c0_ref, h0_ref, cw_ref, cb_ref, wr_ref, br_ref, wi_ref, bi_ref, lam_ref,
                      y_ref, hl_ref, cn_ref):
    t_new = x_ref.shape[0]
    n_tap = cw_ref.shape[0]
    xp = [c0_ref[j] for j in range(n_tap - 1)] + [x_ref[t] for t in range(t_new)]
    h = h0_ref[...]
    for t in range(t_new):
        xc = cb_ref[...]
        for k in range(n_tap):
            xc = xc + xp[t + k] * cw_ref[k:k + 1, :]
        a, bx = _rg_gates(xc, wr_ref[...], br_ref[...], wi_ref[...], bi_ref[...], lam_ref[...])
        h = a * h + bx
        y_ref[t] = h * g_ref[t]
    hl_ref[...] = h
    for j in range(n_tap - 1):
        cn_ref[j] = xp[t_new + j]


def _rg_decode(x_tm, g_tm, conv0_tm, h0, lw):
    weights = [lw["conv_w"], lw["conv_b"], lw["w_r"], lw["b_r"], lw["w_i"], lw["b_i"], lw["rg_lambda"]]
    args = [x_tm, g_tm, conv0_tm, h0] + weights
    return pl.pallas_call(
        _rg_decode_kernel,
        in_specs=[_full(a.shape) for a in args],
        out_specs=[_full(x_tm.shape), _full(h0.shape), _full(conv0_tm.shape)],
        out_shape=[jax.ShapeDtypeStruct(x_tm.shape, F32), jax.ShapeDtypeStruct(h0.shape, F32),
                   jax.ShapeDtypeStruct(conv0_tm.shape, F32)],
        grid=(1,),
        compiler_params=_params(1),
        name="rg_decode",
    )(*args)


def _xattn_kernel(q_ref, k_ref, v_ref, g_ref, o_ref, *, n_heads):
    tq = q_ref.shape[0]
    hd = q_ref.shape[1] // n_heads
    min_rows = 2 * SUBLANES
    for h in range(n_heads):
        sl = slice(h * hd, (h + 1) * hd)
        q = q_ref[:, sl]
        if tq < min_rows:
            q = jnp.concatenate([q.astype(F32), jnp.zeros((min_rows - tq, hd), F32)], axis=0)
        k = k_ref[:, sl].astype(BF16)
        v = v_ref[:, sl].astype(BF16)
        s = _dot_nt(q.astype(BF16), k)
        p = jnp.exp(s - jnp.max(s, axis=-1, keepdims=True))
        o = _dot(p.astype(BF16), v) / jnp.sum(p, axis=-1, keepdims=True)
        o_ref[:, sl] = (o[:tq] * g_ref[:, sl].astype(F32)).astype(o_ref.dtype)


def _xattn_decode_kernel(q_ref, k_ref, v_ref, g_ref, o_ref, *, t_new, n_heads, bb):
    hd = k_ref.shape[1]
    kv_rows = k_ref.shape[0] // bb
    grp = 2 * SUBLANES
    pad = jnp.zeros((grp - t_new, hd), F32)
    row = lax.broadcasted_iota(jnp.int32, (n_heads * grp, kv_rows), 0)
    col = lax.broadcasted_iota(jnp.int32, (n_heads * grp, kv_rows), 1)
    own = col % n_heads == row // grp
    for j in range(bb):
        rows = slice(j * t_new, (j + 1) * t_new)
        k = k_ref[j * kv_rows:(j + 1) * kv_rows, :].astype(BF16)
        v = v_ref[j * kv_rows:(j + 1) * kv_rows, :].astype(BF16)
        pieces = []
        for h in range(n_heads):
            pieces += [q_ref[rows, h * hd:(h + 1) * hd], pad]
        q = jnp.concatenate(pieces, axis=0).astype(BF16)
        s = jnp.where(own, _dot_nt(q, k), NEG_BIG)
        p = jnp.exp(s - jnp.max(s, axis=-1, keepdims=True))
        o = _dot(p.astype(BF16), v) / jnp.sum(p, axis=-1, keepdims=True)
        for h in range(n_heads):
            sl = slice(h * hd, (h + 1) * hd)
            o_ref[rows, sl] = o[h * grp:h * grp + t_new] * g_ref[rows, sl]


def _xattn_decode(qx, cache_k, cache_v, sgx, *, layer, t_new, bb):
    n, xw = qx.shape
    _, bd, n_mem, n_heads, hd = cache_k.shape
    kv_rows = n_mem * n_heads
    rows = pl.BlockSpec((bb * t_new, xw), lambda i: (i, 0))
    mem = pl.BlockSpec((bb * kv_rows, hd), lambda i: (layer * (bd // bb) + i, 0))
    return pl.pallas_call(
        functools.partial(_xattn_decode_kernel, t_new=t_new, n_heads=n_heads, bb=bb),
        grid=(bd // bb,),
        in_specs=[rows, mem, mem, rows],
        out_specs=rows,
        out_shape=jax.ShapeDtypeStruct((n, xw), F32),
        compiler_params=_params(1),
        name="xattn_dec",
    )(qx, cache_k.reshape(-1, hd), cache_v.reshape(-1, hd), sgx)


def _xattn(qx, mem_k, mem_v, sgx, *, batch, seq, n_mem, n_heads, tq, base=0):
    n, xw = qx.shape
    nq = seq // tq
    return pl.pallas_call(
        functools.partial(_xattn_kernel, n_heads=n_heads),
        grid=(batch, nq),
        in_specs=[
            pl.BlockSpec((tq, xw), lambda b, i: (b * nq + i, 0)),
            pl.BlockSpec((n_mem, xw), lambda b, i: (base + b, 0)),
            pl.BlockSpec((n_mem, xw), lambda b, i: (base + b, 0)),
            pl.BlockSpec((tq, xw), lambda b, i: (b * nq + i, 0)),
        ],
        out_specs=pl.BlockSpec((tq, xw), lambda b, i: (b * nq + i, 0)),
        out_shape=jax.ShapeDtypeStruct((n, xw), qx.dtype),
        compiler_params=_params(2),
        name="xattn",
    )(qx, mem_k, mem_v, sgx)


def _out_proj_kernel(*refs, decode, final, n_heads):
    if decode:
        ol_ref, sga_ref, wuv_ref = refs[:3]
        refs = refs[3:]
    else:
        ya_ref = refs[0]
        refs = refs[1:]
    yb_ref, yc_ref, x_ref, woa_ref, wob_ref, woc_ref = refs[:6]
    refs = refs[6:]
    if final:
        fg_ref, xo_ref, yo_ref = refs
    else:
        (xo_ref,) = refs

    acc = x_ref[...] + _dot(yb_ref[...].astype(BF16), wob_ref[...]) + _dot(yc_ref[...].astype(BF16), woc_ref[...])
    if decode:
        rank, dv = wuv_ref.shape[1:]
        for h in range(n_heads):
            o_a = _dot(ol_ref[:, h * rank:(h + 1) * rank].astype(BF16), wuv_ref[h])
            y_a = (o_a * sga_ref[:, h * dv:(h + 1) * dv]).astype(BF16)
            acc = acc + _dot(y_a, woa_ref[h * dv:(h + 1) * dv, :])
    else:
        acc = acc + _dot(ya_ref[...], woa_ref[...])
    xo_ref[...] = acc
    if final:
        yo_ref[...] = _rms(acc, fg_ref[...])


def _out_proj(first, y_b, y_c, x, lw, final_g, *, decode, tm):
    n, d = x.shape
    weights = [lw["w_oa"], lw["w_ob"], lw["w_oc"]]
    final = final_g is not None
    acts = list(first) + [y_b, y_c, x]
    if decode:
        in_specs = [pl.BlockSpec((tm, first[0].shape[1]), lambda i: (i, 0)),
                    pl.BlockSpec((tm, first[1].shape[1]), lambda i: (i, 0)), _full(first[2].shape)]
    else:
        in_specs = [pl.BlockSpec((tm, first[0].shape[1]), lambda i: (i, 0))]
    in_specs += [pl.BlockSpec((tm, a.shape[1]), lambda i: (i, 0)) for a in (y_b, y_c, x)]
    in_specs += [_full(w.shape) for w in weights]
    args = acts + weights
    n_out = 1
    if final:
        in_specs.append(_full(final_g.shape))
        args.append(final_g)
        n_out = 2
    out = pl.pallas_call(
        functools.partial(_out_proj_kernel, decode=decode, final=final, n_heads=lw["n_heads"]),
        grid=(n // tm,),
        in_specs=in_specs,
        out_specs=[pl.BlockSpec((tm, d), lambda i: (i, 0))] * n_out,
        out_shape=[jax.ShapeDtypeStruct((n, d), F32)] * n_out,
        compiler_params=_params(1),
        name="out_proj_dec" if decode else "out_proj",
    )(*args)
    return out if final else (out[0], None)


def _rope_tables(pos, rope, rows):
    half = rope // 2
    inv = 1.0 / (ROPE_BASE ** (jnp.arange(half, dtype=F32) / half))
    ang = pos.astype(F32)[:, None] * inv[None, :]
    cos, sin = jnp.cos(ang), jnp.sin(ang)
    zeros = jnp.zeros((pos.shape[0], LANES - rope), F32)
    cos_t = jnp.concatenate([cos, cos, zeros], axis=1)
    sin_t = jnp.concatenate([-sin, sin, zeros], axis=1)
    reps = max(rows // pos.shape[0], 1)
    return jnp.tile(cos_t, (reps, 1)), jnp.tile(sin_t, (reps, 1))


def _block_diag(w):
    nb, bs, _ = w.shape
    eye = jnp.eye(nb, dtype=w.dtype)
    return (w[:, :, None, :] * eye[:, None, :, None]).reshape(nb * bs, nb * bs)


def _layer_weights(l, ln_g, w_in, q_norm_g, kv_norm_g, w_q_b, w_kv_b, conv_w, conv_b, w_r, b_r, w_i, b_i,
                   rg_lambda, w_out, rope, x_heads):
    d = w_in.shape[1]
    q_rank, n_heads, qk = w_q_b.shape[1:]
    kv_rank = w_kv_b.shape[1]
    nope = qk - rope
    dv = w_kv_b.shape[3] - nope
    rgw = conv_w.shape[2]
    mla_w = n_heads * dv
    xw = (w_in.shape[2] - q_rank - kv_rank - rope - mla_w - 2 * rgw) // 2
    splits, off = [], 0
    for width in (q_rank, kv_rank, rope, mla_w, rgw, rgw, xw, xw):
        splits.append(w_in[l][:, off:off + width].astype(BF16))
        off += width
    w_qlat, w_c, w_kpe, w_ga, w_xrg, w_grg, w_qx, w_gx = splits
    w_qb = jnp.pad(w_q_b[l], ((0, 0), (0, 0), (0, LANES - rope))).reshape(q_rank, n_heads * (nope + LANES))
    wo = w_out[l].astype(BF16)
    return dict(
        n_heads=n_heads, nope=nope, rope=rope, kv_rank=kv_rank, v_head=dv, x_head_dim=xw // x_heads,
        ln_g=ln_g[l][None, :], w_qlat=w_qlat, q_norm_g=q_norm_g[l][None, :], w_qb=w_qb.astype(BF16),
        w_c=w_c, kv_norm_g=kv_norm_g[l][None, :], w_kpe=jnp.pad(w_kpe, ((0, 0), (0, LANES - rope))),
        w_ga=w_ga, w_xrg=w_xrg, w_grg=w_grg, w_qx=w_qx, w_gx=w_gx,
        w_uk=w_kv_b[l][:, :, :nope].reshape(kv_rank, n_heads * nope).astype(BF16),
        w_uv=w_kv_b[l][:, :, nope:].reshape(kv_rank, n_heads * dv).astype(BF16),
        w_ukt=jnp.transpose(w_kv_b[l][:, :, :nope], (1, 2, 0)).astype(BF16),
        w_uvh=jnp.transpose(w_kv_b[l][:, :, nope:], (1, 0, 2)).astype(BF16),
        conv_w=conv_w[l], conv_b=conv_b[l][None, :],
        w_r=_block_diag(w_r[l]).astype(BF16), b_r=b_r[l].reshape(1, rgw),
        w_i=_block_diag(w_i[l]).astype(BF16), b_i=b_i[l].reshape(1, rgw),
        rg_lambda=rg_lambda[l][None, :],
        w_oa=wo[:mla_w], w_ob=wo[mla_w:mla_w + rgw], w_oc=wo[mla_w + rgw:],
    )


def _pick(n, pref):
    t = min(n, pref)
    while n % t:
        t //= 2
    return t


def kernel(x_prompt, x_sample, mem_prompt, cache_ckv, cache_kpe, state_rg_h, state_rg_conv, cache_mem_k, cache_mem_v, page_table, ln_g, w_in, q_norm_g, kv_norm_g, w_q_b, w_kv_b, conv_w, conv_b, w_r, b_r, w_i, b_i, rg_lambda, mem_norm_g, w_mem_kv, w_out, final_norm_g):
    batch, seq, d = x_prompt.shape
    bd, t_new, _ = x_sample.shape
    depth = w_in.shape[0]
    n_mem = mem_prompt.shape[1]
    x_heads, x_hd = cache_mem_k.shape[3:]
    n_pages = page_table.shape[1]
    page = cache_ckv.shape[2]
    rope = cache_kpe.shape[3]
    n_tap = conv_w.shape[1]
    assert t_new >= n_tap - 1 and seq >= SUBLANES and t_new == SUBLANES

    tm_p = _pick(seq, 512)
    tm_d = _pick(bd * t_new, 256)
    tq = _pick(seq, 1024)
    tc = _pick(seq, 256)
    ch = _pick(n_pages, 64)
    cache_kpe_t = jnp.swapaxes(cache_kpe, 2, 3)

    cos_p, sin_p = _rope_tables(jnp.arange(seq, dtype=jnp.int32), rope, tm_p)
    cos_d, sin_d = _rope_tables(n_pages * page + jnp.arange(t_new, dtype=jnp.int32), rope, tm_d)

    xp = x_prompt.reshape(batch * seq, d)
    xs = x_sample.reshape(bd * t_new, d)
    mem = mem_prompt.reshape(batch * n_mem, d)
    fg = final_norm_g[None, :]
    cmk = cache_mem_k.reshape(depth * bd * n_mem, x_heads * x_hd)
    cmv = cache_mem_v.reshape(depth * bd * n_mem, x_heads * x_hd)

    outs = [[] for _ in range(10)]
    yp = ys = None
    for l in range(depth):
        lw = _layer_weights(l, ln_g, w_in, q_norm_g, kv_norm_g, w_q_b, w_kv_b, conv_w, conv_b, w_r, b_r, w_i, b_i,
                            rg_lambda, w_out, rope, x_heads)
        n_heads = lw["n_heads"]
        last = l == depth - 1

        mk, mv = _mem_kv(mem, mem_norm_g[l][None, :], w_mem_kv[l].astype(BF16), tm=_pick(batch * n_mem, 512))
        qf, c_p, kpe_p, kn, kpb, v, sga, xrg, sgr, qx, sgx = _in_proj(xp, lw, cos_p, sin_p, decode=False, tm=tm_p,
                                                                      seq=seq)
        y_a = _attn_prompt(qf, kn, kpb, v, sga, batch=batch, seq=seq, n_heads=n_heads, tq=tq,
                           hp=4 if n_heads % 4 == 0 else 1, nsub=2 if tq % (2 * LANES) == 0 else 1)
        y_b, h_p, cv_p = _rg_prompt(xrg, sgr, lw, batch=batch, seq=seq, tc=tc)
        y_c = _xattn(qx, mk, mv, sgx, batch=batch, seq=seq, n_mem=n_mem, n_heads=x_heads, tq=_pick(seq, 2048))
        xp, yp = _out_proj([y_a], y_b, y_c, xp, lw, fg if last else None, decode=False, tm=tm_p)

        qc, qpe, c_s, kpe_s, sga, xrg, sgr, qx, sgx = _in_proj(xs, lw, cos_d, sin_d, decode=True, tm=tm_d, seq=t_new)
        o_lat = _attn_decode(page_table, qc, qpe, c_s, kpe_s, cache_ckv, cache_kpe_t, layer=l, t_new=t_new,
                             n_heads=n_heads, ch=ch)
        rgw = xrg.shape[1]
        to_tm = lambda a: jnp.transpose(a.reshape(bd, -1, rgw), (1, 0, 2))
        y_tm, h_s, cv_tm = _rg_decode(to_tm(xrg), to_tm(sgr), jnp.transpose(state_rg_conv[l], (1, 0, 2)),
                                      state_rg_h[l], lw)
        y_b = jnp.transpose(y_tm, (1, 0, 2)).reshape(bd * t_new, rgw)
        y_c = _xattn_decode(qx, cache_mem_k, cache_mem_v, sgx, layer=l, t_new=t_new, bb=_pick(bd, 4))
        xs, ys = _out_proj([o_lat, sga, lw["w_uvh"]], y_b, y_c, xs, lw, fg if last else None, decode=True, tm=tm_d)

        for lst, val in zip(outs, (
                c_p.reshape(batch, seq, -1), kpe_p.reshape(batch, seq, -1), h_p[:, 0], cv_p[:, SUBLANES - (n_tap - 1):],
                mk.reshape(batch, n_mem, x_heads, x_hd), mv.reshape(batch, n_mem, x_heads, x_hd),
                c_s.reshape(bd, t_new, -1), kpe_s.reshape(bd, t_new, -1), h_s, jnp.transpose(cv_tm, (1, 0, 2)))):
            lst.append(val)

    return (yp.reshape(batch, seq, d), ys.reshape(bd, t_new, d)) + tuple(jnp.stack(o) for o in outs)
```
